```python
import math
import jax, jax.numpy as jnp
from jax import lax
import numpy as np

D_MODEL = 1024
BATCH = 1
SEQ = 16384
DEPTH = 4
DEC_BATCH = 16
DEC_SEQ = 32
PAST_LEN = 1024

CHUNK = 64
N_EVEN = (DEPTH + 1) // 2
N_ODD = DEPTH // 2
H_A = 8
DA = 64
VA = 2 * DA
WA = H_A * VA
H_B = 8
DH_B = 128
WB = H_B * DH_B
D_RNN = 2 * D_MODEL
H_C = 16
BW = D_RNN // H_C
CONV_W = 4
C_RG = 8.0
QB = 128
EPS = 1e-6

kernel_name = 'hybrid_streaming_diffattn_mlstm_rglru_step'


def rmsnorm(x, w):
    xf = x.astype(jnp.float32)
    y = xf * lax.rsqrt(jnp.mean(xf * xf, axis=-1, keepdims=True) + EPS)
    return (y * w.astype(jnp.float32)).astype(x.dtype)


def adaln(x, c, norm_w, w_ada, b_ada):
    mod = jax.nn.silu(c) @ w_ada + b_ada
    shift, scale, gate = jnp.split(mod, 3, axis=-1)
    h = rmsnorm(x, norm_w) * (1 + scale[:, None, :]) + shift[:, None, :]
    return h, gate[:, None, :]


def diff_weights(s, lam):
    p = jax.nn.softmax(s, axis=-1)
    return p[:, :, 0] - lam * p[:, :, 1]


def diff_attn_prompt(q, k, v, lam):
    B, S = q.shape[:2]
    nq = S // QB
    key_chunk = jnp.arange(S) // CHUNK
    q_blocks = q.reshape(B, nq, QB, H_A, 2, DA).swapaxes(0, 1)
    starts = jnp.arange(nq) * QB

    def block(args):
        qb, start = args
        s = jnp.einsum('bqhcd,bkhcd->bhcqk', qb, k).astype(jnp.float32) * DA ** -0.5
        q_chunk = (start + jnp.arange(QB)) // CHUNK
        s = jnp.where(key_chunk[None, :] <= q_chunk[:, None], s, -jnp.inf)
        a = diff_weights(s, lam).astype(v.dtype)
        return jnp.einsum('bhqk,bkhe->bqhe', a, v)

    o = lax.map(block, (q_blocks, starts))
    return o.swapaxes(0, 1).reshape(B, S, H_A, VA)


def diff_attn_sample(q, k_all, v_all, lam):
    s = jnp.einsum('bqhcd,bkhcd->bhcqk', q, k_all).astype(jnp.float32) * DA ** -0.5
    a = diff_weights(s, lam).astype(v_all.dtype)
    return jnp.einsum('bhqk,bkhe->bqhe', a, v_all)


def mlstm_chunk(state, q, k, v, ig, lf):
    f32 = jnp.float32
    c0, n0, m0 = (t.astype(f32) for t in state)
    q, k, v = q.astype(f32), k.astype(f32), v.astype(f32)
    L = q.shape[1]
    b = jnp.cumsum(lf.astype(f32), axis=1).transpose(0, 2, 1)
    igt = ig.astype(f32).transpose(0, 2, 1)
    causal = jnp.tril(jnp.ones((L, L), dtype=bool))
    dmat = jnp.where(causal, b[..., :, None] - b[..., None, :] + igt[..., None, :], -jnp.inf)
    m_t = jnp.maximum(b + m0[..., None], jnp.max(dmat, axis=-1))
    w = jnp.exp(dmat - m_t[..., None])
    inter = jnp.exp(b + m0[..., None] - m_t)
    s = jnp.einsum('bthd,bshd->bhts', q, k) * w
    num = (jnp.einsum('bhts,bshe->bthe', s, v)
           + inter.transpose(0, 2, 1)[..., None] * jnp.einsum('bthd,bhde->bthe', q, c0))
    den = jnp.sum(s, axis=-1) + inter * jnp.einsum('bthd,bhd->bht', q, n0)
    denom = jnp.maximum(jnp.abs(den), jnp.exp(-m_t)).transpose(0, 2, 1)[..., None]
    h = num / denom
    b_last = b[..., -1]
    m_last = m_t[..., -1]
    ws = jnp.exp(b_last[..., None] - b + igt - m_last[..., None])
    decay = jnp.exp(b_last + m0 - m_last)
    c_new = decay[..., None, None] * c0 + jnp.einsum('bhs,bshd,bshe->bhde', ws, k, v)
    n_new = decay[..., None] * n0 + jnp.einsum('bhs,bshd->bhd', ws, k)
    return (c_new, n_new, m_last), h


def mlstm_prompt(q, k, v, ig, lf):
    B, S, H, Dh = q.shape
    nc = S // CHUNK

    def to_chunks(t):
        return t.reshape((B, nc, CHUNK) + t.shape[2:]).swapaxes(0, 1)

    init = (jnp.zeros((B, H, Dh, Dh), jnp.float32), jnp.zeros((B, H, Dh), jnp.float32),
            jnp.zeros((B, H), jnp.float32))

    def step(state, xs):
        return mlstm_chunk(state, *xs)

    state, h = lax.scan(step, init, (to_chunks(q), to_chunks(k), to_chunks(v), to_chunks(ig), to_chunks(lf)))
    return state, h.swapaxes(0, 1).reshape(B, S, H, Dh)


def causal_conv(xr, buf, w, b):
    T = xr.shape[1]
    xpad = jnp.concatenate([buf.astype(xr.dtype), xr], axis=1)
    y = sum(xpad[:, j:j + T] * w[j] for j in range(CONV_W)) + b
    return y, xpad[:, -(CONV_W - 1):]


def rglru(xc, wa, ba, wx, bx, lam, h0):
    f32 = jnp.float32
    B, T, _ = xc.shape
    xf = xc.astype(f32)
    xh = xf.reshape(B, T, H_C, BW)
    r = jax.nn.sigmoid(jnp.einsum('bthi,hij->bthj', xh, wa.astype(f32)).reshape(B, T, D_RNN) + ba.astype(f32))
    i = jax.nn.sigmoid(jnp.einsum('bthi,hij->bthj', xh, wx.astype(f32)).reshape(B, T, D_RNN) + bx.astype(f32))
    log_a = -C_RG * jax.nn.softplus(-lam.astype(f32)) * r
    a = jnp.exp(log_a)
    u = jnp.sqrt(-jnp.expm1(2.0 * log_a)) * (i * xf)
    u = u.at[:, 0].add(a[:, 0] * h0.astype(f32))
    _, hs = lax.associative_scan(lambda l, rr: (l[0] * rr[0], rr[0] * l[1] + rr[1]), (a, u), axis=1)
    return hs


def even_layer(h, li, e, p, cache):
    B, T, _ = h.shape
    sizes = (WA, WA, WA, WA, WB, WB, WB, WB, WB, H_B, H_B)
    proj = h @ p['w_in_even'][e]
    qa, ka, va, za, qb, kb, vb, ob, zb, ig, fg = jnp.split(proj, np.cumsum(sizes)[:-1].tolist(), axis=-1)
    qa = qa.reshape(B, T, H_A, 2, DA)
    ka = ka.reshape(B, T, H_A, 2, DA)
    va = va.reshape(B, T, H_A, VA)
    lam_init = 0.8 - 0.6 * math.exp(-0.3 * li)
    f32 = jnp.float32
    lam = (jnp.exp(jnp.sum(p['lambda_q1'][e].astype(f32) * p['lambda_k1'][e].astype(f32)))
           - jnp.exp(jnp.sum(p['lambda_q2'][e].astype(f32) * p['lambda_k2'][e].astype(f32))) + lam_init)
    if cache is None:
        oa = diff_attn_prompt(qa, ka, va, lam)
    else:
        k_past, v_past, c0, n0, m0 = cache
        oa = diff_attn_sample(qa, jnp.concatenate([k_past, ka], axis=1),
                              jnp.concatenate([v_past, va], axis=1), lam)
    oa = (rmsnorm(oa, p['subln_w'][e]) * (1.0 - lam_init)).reshape(B, T, WA) * jax.nn.silu(za)
    qb = qb.reshape(B, T, H_B, DH_B)
    kb = kb.reshape(B, T, H_B, DH_B) * DH_B ** -0.5
    vb = vb.reshape(B, T, H_B, DH_B)
    bg = p['b_gates_even'][e]
    ig = ig + bg[:H_B]
    lf = jax.nn.log_sigmoid((fg + bg[H_B:]).astype(f32))
    if cache is None:
        (c_new, n_new, m_new), hb = mlstm_prompt(qb, kb, vb, ig, lf)
    else:
        (c_new, n_new, m_new), hb = mlstm_chunk((c0, n0, m0), qb, kb, vb, ig, lf)
    hb = jax.nn.sigmoid(ob.astype(f32)).reshape(B, T, H_B, DH_B) * hb
    hb = rmsnorm(hb, p['mh_norm_w'][e].reshape(H_B, DH_B)).astype(h.dtype).reshape(B, T, WB)
    hb = hb * jax.nn.silu(zb)
    y = jnp.concatenate([oa, hb], axis=-1) @ p['w_out_even'][e]
    return y, (ka, va, c_new, n_new, m_new)


def odd_layer(h, o, p, cache):
    B, T, _ = h.shape
    xr, zr = jnp.split(h @ p['w_in_odd'][o], 2, axis=-1)
    if cache is None:
        buf = jnp.zeros((B, CONV_W - 1, D_RNN), h.dtype)
        h0 = jnp.zeros((B, D_RNN), jnp.float32)
    else:
        buf, h0 = cache
    xc, conv_state = causal_conv(xr, buf, p['conv_w'][o], p['conv_b'][o])
    hr = rglru(xc, p['rg_wa'][o], p['rg_ba'][o], p['rg_wx'][o], p['rg_bx'][o], p['rg_lambda'][o], h0)
    y = (hr.astype(h.dtype) * jax.nn.silu(zr)) @ p['w_out_odd'][o]
    return y, (conv_state, hr[:, -1])


def trunk(x, c, p, cache):
    even_states, odd_states = [], []
    for li in range(DEPTH):
        h, gate = adaln(x, c, p['norm_w'][li], p['w_ada'][li], p['b_ada'][li])
        if li % 2 == 0:
            e = li // 2
            lc = None if cache is None else (cache['k'][e], cache['v'][e], cache['c'][e], cache['n'][e], cache['m'][e])
            y, st = even_layer(h, li, e, p, lc)
            even_states.append(st)
        else:
            o = li // 2
            lc = None if cache is None else (cache['conv'][o], cache['h'][o])
            y, st = odd_layer(h, o, p, lc)
            odd_states.append(st)
        x = x + gate * y.astype(x.dtype)
    ev = [jnp.stack(s) for s in zip(*even_states)]
    od = [jnp.stack(s) for s in zip(*odd_states)]
    return rmsnorm(x, p['final_w']), ev, od


def setup_inputs(seed: int = 0) -> dict:
    key = jax.random.key(seed)
    ks = iter(jax.random.split(key, 48))
    D = D_MODEL

    def nrm(shape, s):
        return jax.random.normal(next(ks), shape, jnp.float32) * s

    u = jax.random.uniform(next(ks), (N_ODD, D_RNN), jnp.float32, 0.9, 0.999)
    sg = u ** (1.0 / C_RG)
    return {
        'x_prompt': nrm((BATCH, SEQ, D), 1.0),
        'x_sample': nrm((DEC_BATCH, DEC_SEQ, D), 1.0),
        'c_prompt': nrm((BATCH, D), 1.0),
        'c_sample': nrm((DEC_BATCH, D), 1.0),
        'cache_k': nrm((N_EVEN, DEC_BATCH, PAST_LEN, H_A, 2, DA), 1.0),
        'cache_v': nrm((N_EVEN, DEC_BATCH, PAST_LEN, H_A, VA), 1.0),
        'state_mlstm_c': nrm((N_EVEN, DEC_BATCH, H_B, DH_B, DH_B), 0.1),
        'state_mlstm_n': nrm((N_EVEN, DEC_BATCH, H_B, DH_B), 0.1),
        'state_mlstm_m': nrm((N_EVEN, DEC_BATCH, H_B), 0.5),
        'state_conv': nrm((N_ODD, DEC_BATCH, CONV_W - 1, D_RNN), 1.0),
        'state_rglru': nrm((N_ODD, DEC_BATCH, D_RNN), 0.5),
        'norm_w': 1.0 + nrm((DEPTH, D), 0.02),
        'w_ada': nrm((DEPTH, D, 3 * D), 0.5 * D ** -0.5),
        'b_ada': nrm((DEPTH, 3 * D), 0.01),
        'w_in_even': nrm((N_EVEN, D, 4 * WA + 5 * WB + 2 * H_B), D ** -0.5),
        'b_gates_even': jnp.concatenate([nrm((N_EVEN, H_B), 0.1),
                                         jnp.broadcast_to(jnp.linspace(3.0, 6.0, H_B), (N_EVEN, H_B)) + nrm((N_EVEN, H_B), 0.1)], axis=-1),
        'lambda_q1': nrm((N_EVEN, DA), 0.1),
        'lambda_k1': nrm((N_EVEN, DA), 0.1),
        'lambda_q2': nrm((N_EVEN, DA), 0.1),
        'lambda_k2': nrm((N_EVEN, DA), 0.1),
        'subln_w': 1.0 + nrm((N_EVEN, VA), 0.02),
        'mh_norm_w': 1.0 + nrm((N_EVEN, WB), 0.02),
        'w_out_even': nrm((N_EVEN, WA + WB, D), (WA + WB) ** -0.5),
        'w_in_odd': nrm((N_ODD, D, 2 * D_RNN), D ** -0.5),
        'conv_w': nrm((N_ODD, CONV_W, D_RNN), CONV_W ** -0.5),
        'conv_b': nrm((N_ODD, D_RNN), 0.01),
        'rg_wa': nrm((N_ODD, H_C, BW, BW), BW ** -0.5),
        'rg_ba': nrm((N_ODD, D_RNN), 0.01),
        'rg_wx': nrm((N_ODD, H_C, BW, BW), BW ** -0.5),
        'rg_bx': nrm((N_ODD, D_RNN), 0.01),
        'rg_lambda': jnp.log(sg) - jnp.log1p(-sg),
        'w_out_odd': nrm((N_ODD, D_RNN, D), D_RNN ** -0.5),
        'final_w': 1.0 + nrm((D,), 0.02),
    }


def reference(x_prompt, x_sample, c_prompt, c_sample, cache_k, cache_v, state_mlstm_c, state_mlstm_n,
              state_mlstm_m, state_conv, state_rglru, norm_w, w_ada, b_ada, w_in_even, b_gates_even,
              lambda_q1, lambda_k1, lambda_q2, lambda_k2, subln_w, mh_norm_w, w_out_even, w_in_odd,
              conv_w, conv_b, rg_wa, rg_ba, rg_wx, rg_bx, rg_lambda, w_out_odd, final_w):
    p = dict(norm_w=norm_w, w_ada=w_ada, b_ada=b_ada, w_in_even=w_in_even, b_gates_even=b_gates_even,
             lambda_q1=lambda_q1, lambda_k1=lambda_k1, lambda_q2=lambda_q2, lambda_k2=lambda_k2,
             subln_w=subln_w, mh_norm_w=mh_norm_w, w_out_even=w_out_even, w_in_odd=w_in_odd,
             conv_w=conv_w, conv_b=conv_b, rg_wa=rg_wa, rg_ba=rg_ba, rg_wx=rg_wx, rg_bx=rg_bx,
             rg_lambda=rg_lambda, w_out_odd=w_out_odd, final_w=final_w)
    y_prompt, ev_p, od_p = trunk(x_prompt, c_prompt, p, None)
    k_p, v_p, mc_p, mn_p, mm_p = ev_p
    conv_p, h_p = od_p
    cache = dict(k=cache_k, v=cache_v, c=state_mlstm_c, n=state_mlstm_n, m=state_mlstm_m,
                 conv=state_conv, h=state_rglru)
    y_sample, ev_s, od_s = trunk(x_sample, c_sample, p, cache)
    k_s, v_s, mc_s, mn_s, mm_s = ev_s
    conv_s, h_s = od_s
    return (y_prompt, y_sample, k_p, v_p, mc_p, mn_p, mm_p, conv_p, h_p,
            k_s, v_s, mc_s, mn_s, mm_s, conv_s, h_s)
```

```python
import functools
import math

import jax
import jax.numpy as jnp
from jax import lax
from jax.experimental import pallas as pl
from jax.experimental.pallas import tpu as pltpu

F32 = jnp.float32
BF16 = jnp.bfloat16
EPS = 1e-6
CHUNK = 64
C_RG = 8.0
CONV_W = 4
LANES = 128
SUBLANES = 8
NT_DIMS = (((1,), (1,)), ((), ()))
TN_DIMS = (((0,), (0,)), ((), ()))
HIGHEST = lax.Precision.HIGHEST
VMEM_LIMIT = 56 * 1024 * 1024


def _params(sem):
    return pltpu.CompilerParams(dimension_semantics=sem, vmem_limit_bytes=VMEM_LIMIT)


def _silu(z):
    return z * jax.nn.sigmoid(z)


def _mod_kernel(c_ref, w_ref, b_ref, o_ref):
    sc = _silu(c_ref[...]).astype(BF16)
    o_ref[0] = jnp.dot(sc, w_ref[0].astype(BF16), preferred_element_type=F32) + b_ref[0]


def _modulation(c_all, w_ada, b_ada):
    depth, d, n = w_ada.shape
    r = c_all.shape[0]
    tn = 1024
    return pl.pallas_call(
        _mod_kernel,
        grid=(depth, n // tn),
        in_specs=[
            pl.BlockSpec((r, d), lambda l, j: (0, 0)),
            pl.BlockSpec((1, d, tn), lambda l, j: (l, 0, j)),
            pl.BlockSpec((1, 1, tn), lambda l, j: (l, 0, j)),
        ],
        out_specs=pl.BlockSpec((1, r, tn), lambda l, j: (l, 0, j)),
        out_shape=jax.ShapeDtypeStruct((depth, r, n), F32),
        compiler_params=_params(("arbitrary", "arbitrary")),
        name="adaln_mod",
    )(c_all, w_ada, b_ada.reshape(depth, 1, n))


def _norm_kernel(x_ref, nw_ref, sc_ref, sh_ref, o_ref):
    x = x_ref[...]
    ms = jnp.mean(x * x, axis=-1, keepdims=True)
    y = (x * lax.rsqrt(ms + EPS)) * nw_ref[...]
    o_ref[...] = (y * (1.0 + sc_ref[...]) + sh_ref[...]).astype(o_ref.dtype)


def _row_spec(arr, tm):
    if arr.shape[0] == 1:
        return pl.BlockSpec((1, arr.shape[1]), lambda i: (0, 0))
    return pl.BlockSpec((tm, arr.shape[1]), lambda i: (i, 0))


def _norm(x, nw, scale, shift, out_dtype, tm):
    m, d = x.shape
    return pl.pallas_call(
        _norm_kernel,
        grid=(m // tm,),
        in_specs=[
            pl.BlockSpec((tm, d), lambda i: (i, 0)),
            pl.BlockSpec((1, d), lambda i: (0, 0)),
            _row_spec(scale, tm),
            _row_spec(shift, tm),
        ],
        out_specs=pl.BlockSpec((tm, d), lambda i: (i, 0)),
        out_shape=jax.ShapeDtypeStruct((m, d), out_dtype),
        compiler_params=_params(("arbitrary",)),
        name="adaln_norm",
    )(x, nw.reshape(1, d), scale, shift)


def _proj_acc(h_ref, w_ref, wbf_ref):
    @pl.when(pl.program_id(1) == 0)
    def _():
        wbf_ref[...] = w_ref[...].astype(BF16)

    return jnp.dot(h_ref[...], wbf_ref[...], preferred_element_type=F32)


def _proj_f32_kernel(h_ref, w_ref, o_ref, wbf_ref):
    o_ref[...] = _proj_acc(h_ref, w_ref, wbf_ref)


def _proj_bias_kernel(h_ref, w_ref, b_ref, o_ref, wbf_ref):
    o_ref[...] = _proj_acc(h_ref, w_ref, wbf_ref) + b_ref[...]


def _proj_bf16_kernel(h_ref, w_ref, o_ref, wbf_ref, *, scales):
    acc = _proj_acc(h_ref, w_ref, wbf_ref)
    j = pl.program_id(0)
    s = jnp.float32(scales[0])
    for t in range(1, len(scales)):
        s = jnp.where(j == t, jnp.float32(scales[t]), s)
    o_ref[...] = (acc * s).astype(BF16)


def _proj_k_kernel(h_ref, w_ref, o_ref, obf_ref, wbf_ref):
    acc = _proj_acc(h_ref, w_ref, wbf_ref)
    o_ref[...] = acc
    obf_ref[...] = acc.astype(BF16)


def _proj_vt_kernel(h_ref, w_ref, o_ref, ot_ref, wbf_ref, *, tk):
    acc = _proj_acc(h_ref, w_ref, wbf_ref)
    o_ref[...] = acc
    at = acc.T.astype(BF16)
    nh, nt = ot_ref.shape[0], ot_ref.shape[1]
    va = ot_ref.shape[2]
    for hh in range(nh):
        for t in range(nt):
            ot_ref[hh, t] = at[hh * va:(hh + 1) * va, t * tk:(t + 1) * tk]


def _q_split(acc, axis, da, qk_scale):
    idx = lax.broadcasted_iota(jnp.int32, acc.shape, axis)
    first = (idx % (2 * da)) < da
    a = acc * qk_scale
    zero = jnp.zeros_like(a)
    return jnp.where(first, a, zero).astype(BF16), jnp.where(first, zero, a).astype(BF16)


def _proj_q_kernel(h_ref, w_ref, o1_ref, o2_ref, wbf_ref, *, da, qk_scale):
    acc = _proj_acc(h_ref, w_ref, wbf_ref)
    o1_ref[...], o2_ref[...] = _q_split(acc, 1, da, qk_scale)


def _proj_qt_kernel(h_ref, w_ref, o1_ref, o2_ref, wbf_ref, *, da, qk_scale):
    acc = _proj_acc(h_ref, w_ref, wbf_ref)
    o1_ref[...], o2_ref[...] = _q_split(acc.T, 0, da, qk_scale)


def _col_index(cols, j):
    idx = jnp.int32(cols[0])
    for t in range(1, len(cols)):
        idx = jnp.where(j == t, jnp.int32(cols[t]), idx)
    return idx


def _proj_call(body, h, w, cols, tn, tm, out_shapes, out_specs, extra_in=(), extra_specs=(), name="proj"):
    m, d = h.shape
    nj = len(cols)
    return pl.pallas_call(
        body,
        grid=(nj, m // tm),
        in_specs=[
            pl.BlockSpec((tm, d), lambda j, i: (i, 0)),
            pl.BlockSpec((d, tn), lambda j, i: (0, _col_index(cols, j))),
            *extra_specs,
        ],
        out_specs=out_specs,
        out_shape=out_shapes,
        scratch_shapes=[pltpu.VMEM((d, tn), BF16)],
        compiler_params=_params(("arbitrary", "arbitrary")),
        name=name,
    )(h, w, *extra_in)


def _proj_f32(h, w, cols, tn, tm):
    m = h.shape[0]
    return _proj_call(
        _proj_f32_kernel, h, w, cols, tn, tm,
        jax.ShapeDtypeStruct((len(cols), m, tn), F32),
        pl.BlockSpec((None, tm, tn), lambda j, i: (j, i, 0)), name="proj_f32")


def _proj_bf16(h, w, cols, scales, tn, tm):
    m = h.shape[0]
    return _proj_call(
        functools.partial(_proj_bf16_kernel, scales=tuple(scales)), h, w, cols, tn, tm,
        jax.ShapeDtypeStruct((len(cols), m, tn), BF16),
        pl.BlockSpec((None, tm, tn), lambda j, i: (j, i, 0)), name="proj_bf16")


def _proj_bias(h, w_small, bias, tm):
    m = h.shape[0]
    tn = w_small.shape[1]
    return _proj_call(
        _proj_bias_kernel, h, w_small, (0,), tn, tm,
        jax.ShapeDtypeStruct((m, tn), F32),
        pl.BlockSpec((tm, tn), lambda j, i: (i, 0)),
        extra_in=(bias,), extra_specs=(pl.BlockSpec((1, tn), lambda j, i: (0, 0)),), name="proj_gates")


def _proj_k(h, w, col, tn, tm):
    m = h.shape[0]
    return _proj_call(
        _proj_k_kernel, h, w, (col,), tn, tm,
        (jax.ShapeDtypeStruct((m, tn), F32), jax.ShapeDtypeStruct((m, tn), BF16)),
        (pl.BlockSpec((tm, tn), lambda j, i: (i, 0)), pl.BlockSpec((tm, tn), lambda j, i: (i, 0))),
        name="proj_k")


def _proj_vt(h, w, col, tn, tm, heads, tk):
    m = h.shape[0]
    va = tn // heads
    return _proj_call(
        functools.partial(_proj_vt_kernel, tk=tk), h, w, (col,), tn, tm,
        (jax.ShapeDtypeStruct((m, tn), F32), jax.ShapeDtypeStruct((heads, m // tk, va, tk), BF16)),
        (pl.BlockSpec((tm, tn), lambda j, i: (i, 0)),
         pl.BlockSpec((heads, tm // tk, va, tk), lambda j, i: (0, i, 0, 0))),
        name="proj_vt")


def _proj_q(h, w, col, tn, tm, da, transposed):
    m = h.shape[0]
    qk_scale = float(da) ** -0.5
    if transposed:
        body = functools.partial(_proj_qt_kernel, da=da, qk_scale=qk_scale)
        shape = jax.ShapeDtypeStruct((tn, m), BF16)
        spec = pl.BlockSpec((tn, tm), lambda j, i: (0, i))
    else:
        body = functools.partial(_proj_q_kernel, da=da, qk_scale=qk_scale)
        shape = jax.ShapeDtypeStruct((m, tn), BF16)
        spec = pl.BlockSpec((tm, tn), lambda j, i: (i, 0))
    return _proj_call(body, h, w, (col,), tn, tm, (shape, shape), (spec, spec), name="proj_q")


def _attn_finish(o, z, sw, post):
    ms = jnp.mean(o * o, axis=-1, keepdims=True)
    on = (o * lax.rsqrt(ms + EPS)) * sw
    return (on * post) * _silu(z)


def _attn_prompt_kernel(sc_ref, q1_ref, q2_ref, k_ref, vt_ref, za_ref, sw_ref, o_ref,
                        m_ref, l_ref, acc_ref, *, tq, tk):
    i = pl.program_id(1)
    qs = (q1_ref[...], q2_ref[...])
    m_ref[...] = jnp.full(m_ref.shape, -jnp.inf, F32)
    l_ref[...] = jnp.zeros(l_ref.shape, F32)
    acc_ref[...] = jnp.zeros(acc_ref.shape, F32)

    def step(j, masked):
        kt = k_ref[pl.ds(pl.multiple_of(j * tk, tk), tk), :]
        vt = vt_ref[j]
        for c in range(2):
            s = jnp.dot(kt, qs[c], preferred_element_type=F32)
            if masked:
                kc = lax.broadcasted_iota(jnp.int32, (tk, tq), 0) // CHUNK
                qc = lax.broadcasted_iota(jnp.int32, (tk, tq), 1) // CHUNK
                s = jnp.where(kc <= qc, s, -jnp.inf)
            m_old = m_ref[c]
            m_new = jnp.maximum(m_old, jnp.max(s, axis=0, keepdims=True))
            alpha = jnp.exp(m_old - m_new)
            p = jnp.exp(s - m_new)
            l_ref[c] = alpha * l_ref[c] + jnp.sum(p, axis=0, keepdims=True)
            acc_ref[c] = alpha * acc_ref[c] + jnp.dot(vt, p.astype(BF16), preferred_element_type=F32)
            m_ref[c] = m_new

    def body(j, carry):
        step(j, False)
        return carry

    lax.fori_loop(0, i, body, 0)
    step(i, True)

    lam = sc_ref[0, 0]
    post = sc_ref[0, 1]
    o = acc_ref[0] / l_ref[0] - lam * (acc_ref[1] / l_ref[1])
    o_ref[...] = _attn_finish(o.T, za_ref[...], sw_ref[...], post).astype(o_ref.dtype)


def _attn_prompt(scal, q1t, q2t, kbf, vt, fz, sw, heads, tq):
    dq, m = q1t.shape
    hd = dq // heads
    tk = vt.shape[3]
    assert tq == tk and m % tq == 0
    nk = m // tk
    return pl.pallas_call(
        functools.partial(_attn_prompt_kernel, tq=tq, tk=tk),
        grid=(heads, m // tq),
        in_specs=[
            pl.BlockSpec(memory_space=pltpu.SMEM),
            pl.BlockSpec((hd, tq), lambda h, i: (h, i)),
            pl.BlockSpec((hd, tq), lambda h, i: (h, i)),
            pl.BlockSpec((m, hd), lambda h, i: (0, h)),
            pl.BlockSpec((None, nk, hd, tk), lambda h, i: (h, 0, 0, 0)),
            pl.BlockSpec((None, tq, hd), lambda h, i: (0, i, h)),
            pl.BlockSpec((1, hd), lambda h, i: (0, 0)),
        ],
        out_specs=pl.BlockSpec((tq, hd), lambda h, i: (i, h)),
        out_shape=jax.ShapeDtypeStruct((m, dq), BF16),
        scratch_shapes=[
            pltpu.VMEM((2, 1, tq), F32),
            pltpu.VMEM((2, 1, tq), F32),
            pltpu.VMEM((2, hd, tq), F32),
        ],
        compiler_params=_params(("arbitrary", "arbitrary")),
        name="diff_attn_prompt",
    )(scal, q1t, q2t, kbf, vt, fz, sw)


def _attn_sample_kernel(sc_ref, q1_ref, q2_ref, ck_ref, cv_ref, kn_ref, vn_ref, za_ref, sw_ref, o_ref):
    ck = ck_ref[...].astype(BF16)
    cv = cv_ref[...].astype(BF16)
    kn = kn_ref[...].astype(BF16)
    vn = vn_ref[...].astype(BF16)
    outs = []
    for q_ref in (q1_ref, q2_ref):
        q = q_ref[...]
        sc = lax.dot_general(q, ck, NT_DIMS, preferred_element_type=F32)
        sn = lax.dot_general(q, kn, NT_DIMS, preferred_element_type=F32)
        mx = jnp.maximum(jnp.max(sc, axis=-1, keepdims=True), jnp.max(sn, axis=-1, keepdims=True))
        pc = jnp.exp(sc - mx)
        pn = jnp.exp(sn - mx)
        l = jnp.sum(pc, axis=-1, keepdims=True) + jnp.sum(pn, axis=-1, keepdims=True)
        acc = (jnp.dot(pc.astype(BF16), cv, preferred_element_type=F32)
               + jnp.dot(pn.astype(BF16), vn, preferred_element_type=F32))
        outs.append(acc / l)
    lam = sc_ref[0, 0]
    post = sc_ref[0, 1]
    o = outs[0] - lam * outs[1]
    o_ref[...] = _attn_finish(o, za_ref[...], sw_ref[...], post).astype(o_ref.dtype)


def _attn_sample(scal, q1, q2, ck, cv, kv, fz, sw, heads, t):
    m, dq = q1.shape
    hd = dq // heads
    b, p, _ = ck.shape
    row = lambda bb, h: (bb, h)
    return pl.pallas_call(
        _attn_sample_kernel,
        grid=(b, heads),
        in_specs=[
            pl.BlockSpec(memory_space=pltpu.SMEM),
            pl.BlockSpec((t, hd), row),
            pl.BlockSpec((t, hd), row),
            pl.BlockSpec((None, p, hd), lambda bb, h: (bb, 0, h)),
            pl.BlockSpec((None, p, hd), lambda bb, h: (bb, 0, h)),
            pl.BlockSpec((None, t, hd), lambda bb, h: (0, bb, h)),
            pl.BlockSpec((None, t, hd), lambda bb, h: (1, bb, h)),
            pl.BlockSpec((None, t, hd), lambda bb, h: (0, bb, h)),
            pl.BlockSpec((1, hd), lambda bb, h: (0, 0)),
        ],
        out_specs=pl.BlockSpec((t, hd), row),
        out_shape=jax.ShapeDtypeStruct((m, dq), BF16),
        compiler_params=_params(("arbitrary", "arbitrary")),
        name="diff_attn_sample",
    )(scal, q1, q2, ck, cv, kv, kv, fz, sw)


def _mlstm_kernel(q_ref, k_ref, v_ref, ob_ref, zb_ref, g_ref, c0_ref, n0_ref, m0_ref, w_ref,
                  hb_ref, c_out, n_out, m_out, c_scr, n_scr, m_scr, *, L, cpb, H, DH):
    t = pl.program_id(1)

    @pl.when(t == 0)
    def _():
        c_scr[...] = c0_ref[0]
        n_scr[...] = n0_ref[0]
        m_scr[...] = m0_ref[0]

    row = lax.broadcasted_iota(jnp.int32, (L, L), 0)
    col = lax.broadcasted_iota(jnp.int32, (L, L), 1)
    causal = col <= row
    tril = causal.astype(F32)
    sel = (lax.broadcasted_iota(jnp.int32, (2 * H, LANES), 0)
           == lax.broadcasted_iota(jnp.int32, (2 * H, LANES), 1)).astype(F32)
    lane = lax.broadcasted_iota(jnp.int32, (1, LANES), 1)

    for cc in range(cpb):
        r0 = cc * L
        g = g_ref[r0:r0 + L, :]
        lf = jax.nn.log_sigmoid(g)
        bcum = jnp.dot(tril, lf, precision=HIGHEST, preferred_element_type=F32)
        x = jnp.where(lane < H, g, bcum)
        xt = lax.dot_general(sel, x, NT_DIMS, precision=HIGHEST, preferred_element_type=F32)
        m_all = m_scr[...]
        m_next = m_all
        for h in range(H):
            ig_c = g[:, h:h + 1]
            b_c = bcum[:, H + h:H + h + 1]
            ig_r = xt[h:h + 1, :]
            b_r = xt[H + h:H + h + 1, :]
            m0 = m_all[:, h:h + 1]
            dmat = jnp.where(causal, b_c - b_r + ig_r, -jnp.inf)
            m_t = jnp.maximum(b_c + m0, jnp.max(dmat, axis=-1, keepdims=True))
            wgt = jnp.exp(dmat - m_t)
            inter = jnp.exp(b_c + m0 - m_t)
            hs = slice(h * DH, (h + 1) * DH)
            qh = q_ref[r0:r0 + L, hs]
            kh = k_ref[r0:r0 + L, hs]
            vh = v_ref[r0:r0 + L, hs]
            s = lax.dot_general(qh, kh, NT_DIMS, preferred_element_type=F32) * wgt
            c_h = c_scr[h]
            n_h = n_scr[h:h + 1, :]
            num = (jnp.dot(s.astype(BF16), vh, preferred_element_type=F32)
                   + inter * jnp.dot(qh, c_h.astype(BF16), preferred_element_type=F32))
            den = (jnp.sum(s, axis=-1, keepdims=True)
                   + inter * jnp.sum(qh.astype(F32) * n_h, axis=-1, keepdims=True))
            denom = jnp.maximum(jnp.abs(den), jnp.exp(-m_t))
            hh = jax.nn.sigmoid(ob_ref[r0:r0 + L, hs]) * (num / denom)
            ms = jnp.mean(hh * hh, axis=-1, keepdims=True)
            hn = (hh * lax.rsqrt(ms + EPS)) * w_ref[:, hs]
            hb_ref[r0:r0 + L, hs] = (hn * _silu(zb_ref[r0:r0 + L, hs])).astype(hb_ref.dtype)
            b_last = b_c[L - 1:L, :]
            m_last = m_t[L - 1:L, :]
            ws = jnp.exp(b_last - b_c + ig_c - m_last)
            decay = jnp.exp(b_last + m0 - m_last)
            wv = (ws * vh.astype(F32)).astype(BF16)
            c_scr[h] = decay * c_h + lax.dot_general(kh, wv, TN_DIMS, preferred_element_type=F32)
            n_scr[h:h + 1, :] = decay * n_h + jnp.sum(ws * kh.astype(F32), axis=0, keepdims=True)
            m_next = jnp.where(lane == h, m_last, m_next)
        m_scr[...] = m_next

    @pl.when(t == pl.num_programs(1) - 1)
    def _():
        c_out[0] = c_scr[...]
        n_out[0] = n_scr[...]
        m_out[0] = m_scr[...]


def _mlstm(qkv, fz, gates, c0, n0, m0, mh_w, batch, L, cpb):
    _, m, wb = qkv.shape
    _, H, DH, _ = c0.shape
    rows = L * cpb
    nt = m // batch // rows
    blk = lambda which: pl.BlockSpec((None, rows, wb), lambda b, t: (which, b * nt + t, 0))
    return pl.pallas_call(
        functools.partial(_mlstm_kernel, L=L, cpb=cpb, H=H, DH=DH),
        grid=(batch, nt),
        in_specs=[
            blk(0), blk(1), blk(2),
            blk(1), blk(2),
            pl.BlockSpec((rows, LANES), lambda b, t: (b * nt + t, 0)),
            pl.BlockSpec((1, H, DH, DH), lambda b, t: (b, 0, 0, 0)),
            pl.BlockSpec((1, H, DH), lambda b, t: (b, 0, 0)),
            pl.BlockSpec((1, 1, LANES), lambda b, t: (b, 0, 0)),
            pl.BlockSpec((1, wb), lambda b, t: (0, 0)),
        ],
        out_specs=(
            pl.BlockSpec((rows, wb), lambda b, t: (b * nt + t, 0)),
            pl.BlockSpec((1, H, DH, DH), lambda b, t: (b, 0, 0, 0)),
            pl.BlockSpec((1, H, DH), lambda b, t: (b, 0, 0)),
            pl.BlockSpec((1, 1, LANES), lambda b, t: (b, 0, 0)),
        ),
        out_shape=(
            jax.ShapeDtypeStruct((m, wb), BF16),
            jax.ShapeDtypeStruct((batch, H, DH, DH), F32),
            jax.ShapeDtypeStruct((batch, H, DH), F32),
            jax.ShapeDtypeStruct((batch, 1, LANES), F32),
        ),
        scratch_shapes=[
            pltpu.VMEM((H, DH, DH), F32),
            pltpu.VMEM((H, DH), F32),
            pltpu.VMEM((1, LANES), F32),
        ],
        compiler_params=_params(("arbitrary", "arbitrary")),
        name="mlstm_chunks",
    )(qkv, qkv, qkv, fz, fz, gates, c0, n0, m0, mh_w)


def _outproj_kernel(*refs, n_in):
    a_refs = refs[:n_in]
    w_ref, x_ref, g_ref, o_ref, wbf_ref = refs[n_in:]

    @pl.when(pl.program_id(0) == 0)
    def _():
        wbf_ref[...] = w_ref[...].astype(BF16)

    acc = None
    off = 0
    for a_ref in a_refs:
        kk = a_ref.shape[1]
        part = jnp.dot(a_ref[...], wbf_ref[off:off + kk, :], preferred_element_type=F32)
        acc = part if acc is None else acc + part
        off += kk
    o_ref[...] = x_ref[...] + g_ref[...] * acc


def _outproj(acts, w, x, gate, tm):
    m, d = x.shape
    kt = w.shape[0]
    return pl.pallas_call(
        functools.partial(_outproj_kernel, n_in=len(acts)),
        grid=(m // tm,),
        in_specs=[
            *[pl.BlockSpec((tm, a.shape[1]), lambda i: (i, 0)) for a in acts],
            pl.BlockSpec((kt, d), lambda i: (0, 0)),
            pl.BlockSpec((tm, d), lambda i: (i, 0)),
            _row_spec(gate, tm),
        ],
        out_specs=pl.BlockSpec((tm, d), lambda i: (i, 0)),
        out_shape=jax.ShapeDtypeStruct((m, d), F32),
        scratch_shapes=[pltpu.VMEM((kt, d), BF16)],
        compiler_params=_params(("arbitrary",)),
        name="outproj_residual",
    )(*acts, w, x, gate)


def _rglru_kernel(xr_ref, zr_ref, cw_ref, cb_ref, wg_ref, ba_ref, bx_ref, lam_ref, buf0_ref, h0_ref,
                  y_ref, cs_ref, hl_ref, xbuf, a_scr, u_scr, hcar, *, tt, nblk, bw):
    t = pl.program_id(1)
    pad = SUBLANES

    @pl.when(t == 0)
    def _():
        xbuf[0:pad, :] = buf0_ref[0]
        hcar[...] = h0_ref[0]

    xbuf[pad:pad + tt, :] = xr_ref[...]
    xc = cb_ref[...] + cw_ref[CONV_W - 1:CONV_W, :] * xbuf[pad:pad + tt, :]
    for j in range(1, CONV_W):
        xc = xc + cw_ref[CONV_W - 1 - j:CONV_W - j, :] * xbuf[pad - j:pad - j + tt, :]

    neg_c = -C_RG * jax.nn.softplus(-lam_ref[...])
    for hb in range(nblk):
        cs = slice(hb * bw, (hb + 1) * bw)
        xb = xc[:, cs]
        gg = jnp.dot(xb.astype(BF16), wg_ref[hb], preferred_element_type=F32)
        r = jax.nn.sigmoid(gg[:, :bw] + ba_ref[:, cs])
        ig = jax.nn.sigmoid(gg[:, bw:] + bx_ref[:, cs])
        log_a = neg_c[:, cs] * r
        th = jnp.tanh(log_a)
        one_minus_a2 = (-2.0 * th) / (1.0 - th)
        a_scr[:, cs] = jnp.exp(log_a)
        u_scr[:, cs] = jnp.sqrt(one_minus_a2) * (ig * xb)

    rowi = lax.broadcasted_iota(jnp.int32, (SUBLANES, a_scr.shape[1]), 0)

    def group(gi, hc):
        r0 = pl.multiple_of(gi * SUBLANES, SUBLANES)
        a = a_scr[pl.ds(r0, SUBLANES), :]
        u = u_scr[pl.ds(r0, SUBLANES), :]
        for dd in (1, 2, 4):
            ok = rowi >= dd
            u = jnp.where(ok, a * pltpu.roll(u, dd, 0) + u, u)
            a = jnp.where(ok, a * pltpu.roll(a, dd, 0), a)
        hgrp = u + a * hc
        u_scr[pl.ds(r0, SUBLANES), :] = hgrp
        return hgrp[SUBLANES - 1:SUBLANES, :]

    hc = lax.fori_loop(0, tt // SUBLANES, group, hcar[...])
    hcar[...] = hc
    y_ref[...] = (u_scr[...] * _silu(zr_ref[...])).astype(y_ref.dtype)
    xbuf[0:pad, :] = xbuf[tt:tt + pad, :]

    @pl.when(t == pl.num_programs(1) - 1)
    def _():
        cs_ref[0] = xbuf[0:pad, :]
        hl_ref[0] = hc


def _rglru(xz, cw, cb, wg, ba, bx, lam, buf0, h0, batch, tt):
    _, m, dr = xz.shape
    nblk, bw, _ = wg.shape
    nt = m // batch // tt
    vec = lambda: pl.BlockSpec((1, dr), lambda b, t: (0, 0))
    return pl.pallas_call(
        functools.partial(_rglru_kernel, tt=tt, nblk=nblk, bw=bw),
        grid=(batch, nt),
        in_specs=[
            pl.BlockSpec((None, tt, dr), lambda b, t: (0, b * nt + t, 0)),
            pl.BlockSpec((None, tt, dr), lambda b, t: (1, b * nt + t, 0)),
            pl.BlockSpec((CONV_W, dr), lambda b, t: (0, 0)),
            vec(),
            pl.BlockSpec((nblk, bw, 2 * bw), lambda b, t: (0, 0, 0)),
            vec(), vec(), vec(),
            pl.BlockSpec((1, SUBLANES, dr), lambda b, t: (b, 0, 0)),
            pl.BlockSpec((1, 1, dr), lambda b, t: (b, 0, 0)),
        ],
        out_specs=(
            pl.BlockSpec((tt, dr), lambda b, t: (b * nt + t, 0)),
            pl.BlockSpec((1, SUBLANES, dr), lambda b, t: (b, 0, 0)),
            pl.BlockSpec((1, 1, dr), lambda b, t: (b, 0, 0)),
        ),
        out_shape=(
            jax.ShapeDtypeStruct((m, dr), BF16),
            jax.ShapeDtypeStruct((batch, SUBLANES, dr), F32),
            jax.ShapeDtypeStruct((batch, 1, dr), F32),
        ),
        scratch_shapes=[
            pltpu.VMEM((tt + SUBLANES, dr), F32),
            pltpu.VMEM((tt, dr), F32),
            pltpu.VMEM((tt, dr), F32),
            pltpu.VMEM((1, dr), F32),
        ],
        compiler_params=_params(("arbitrary", "arbitrary")),
        name="rglru_block",
    )(xz, xz, cw, cb, wg, ba, bx, lam, buf0, h0)


def _tile(m, pref):
    t = min(m, pref)
    assert m % t == 0
    return t


def _trunk(x3, mods, p, cache, prompt):
    b, t, d = x3.shape
    m = b * t
    depth = p["norm_w"].shape[0]
    x = x3.reshape(m, d)
    tm = _tile(m, 1024)
    va = p["subln_w"].shape[1]
    da = va // 2
    wb = p["mh_norm_w"].shape[1]
    tn = p["w_out_even"].shape[1] - wb
    assert tn == wb
    heads = tn // va
    hb_heads = p["b_gates_even"].shape[1] // 2
    tq = 256
    k_list, v_list, mc_list, mn_list, mm_list, conv_list, hl_list = [], [], [], [], [], [], []

    def rows(v):
        if b == 1:
            return v
        return jnp.broadcast_to(v[:, None, :], (b, t, d)).reshape(m, d)

    for li in range(depth):
        shift, scale, gate = mods[li]
        h = _norm(x, p["norm_w"][li], rows(scale), rows(shift), BF16, tm)
        if li % 2 == 0:
            e = li // 2
            w_in = p["w_in_even"][e]
            lam_init = 0.8 - 0.6 * math.exp(-0.3 * li)
            lam = (jnp.exp(jnp.sum(p["lambda_q1"][e] * p["lambda_k1"][e]))
                   - jnp.exp(jnp.sum(p["lambda_q2"][e] * p["lambda_k2"][e])) + lam_init)
            scal = jnp.stack([lam, jnp.float32(1.0 - lam_init)]).reshape(1, 2).astype(F32)
            sw = p["subln_w"][e].reshape(1, -1)
            fz = _proj_f32(h, w_in, (3, 7, 8), tn, tm)
            qkv_b = _proj_bf16(h, w_in, (4, 5, 6), (1.0, float(LANES) ** -0.5, 1.0), tn, tm)
            ng = 2 * hb_heads
            w_g = jnp.pad(w_in[:, 9 * tn:9 * tn + ng], ((0, 0), (0, LANES - ng)))
            b_g = jnp.pad(p["b_gates_even"][e], (0, LANES - ng)).reshape(1, LANES)
            gates = _proj_bias(h, w_g, b_g, tm)
            if prompt:
                q1, q2 = _proj_q(h, w_in, 0, tn, tm, da, True)
                k_f, k_b = _proj_k(h, w_in, 1, tn, tm)
                v_f, v_t = _proj_vt(h, w_in, 2, tn, tm, heads, tq)
                oa = _attn_prompt(scal, q1, q2, k_b, v_t, fz, sw, heads, tq)
                dh = wb // hb_heads
                c0 = jnp.zeros((b, hb_heads, dh, dh), F32)
                n0 = jnp.zeros((b, hb_heads, dh), F32)
                m0 = jnp.zeros((b, 1, LANES), F32)
                L, cpb = CHUNK, 2
            else:
                q1, q2 = _proj_q(h, w_in, 0, tn, tm, da, False)
                kv = _proj_f32(h, w_in, (1, 2), tn, tm)
                k_f, v_f = kv[0], kv[1]
                ck = cache["k"][e].reshape(b, -1, heads * 2 * da)
                cv = cache["v"][e].reshape(b, -1, heads * 2 * da)
                oa = _attn_sample(scal, q1, q2, ck, cv, kv, fz, sw, heads, t)
                c0 = cache["c"][e]
                n0 = cache["n"][e]
                m0 = jnp.pad(cache["m"][e], ((0, 0), (0, LANES - hb_heads))).reshape(b, 1, LANES)
                L, cpb = t, 1
            hb, c_n, n_n, m_n = _mlstm(qkv_b, fz, gates, c0, n0, m0,
                                       p["mh_norm_w"][e].reshape(1, -1), b, L, cpb)
            k_list.append(k_f.reshape(b, t, heads, 2, da))
            v_list.append(v_f.reshape(b, t, heads, 2 * da))
            mc_list.append(c_n)
            mn_list.append(n_n)
            mm_list.append(m_n[:, 0, :hb_heads])
            x = _outproj([oa, hb], p["w_out_even"][e], x, rows(gate), tm)
        else:
            o = li // 2
            w_in = p["w_in_odd"][o]
            dr = w_in.shape[1] // 2
            xz = _proj_f32(h, w_in, (0, 1), dr, _tile(m, 512))
            wg = jnp.concatenate([p["rg_wa"][o], p["rg_wx"][o]], axis=-1).astype(BF16)
            if cache is None:
                buf0 = jnp.zeros((b, SUBLANES, dr), F32)
                h0 = jnp.zeros((b, 1, dr), F32)
            else:
                buf0 = jnp.pad(cache["conv"][o], ((0, 0), (SUBLANES - (CONV_W - 1), 0), (0, 0)))
                h0 = cache["h"][o].reshape(b, 1, dr)
            y, cs, hl = _rglru(xz, p["conv_w"][o], p["conv_b"][o].reshape(1, dr), wg,
                               p["rg_ba"][o].reshape(1, dr), p["rg_bx"][o].reshape(1, dr),
                               p["rg_lambda"][o].reshape(1, dr), buf0, h0, b, _tile(t, 256))
            conv_list.append(cs[:, SUBLANES - (CONV_W - 1):, :])
            hl_list.append(hl[:, 0, :])
            x = _outproj([y], p["w_out_odd"][o], x, rows(gate), tm)
    zero = jnp.zeros((1, d), F32)
    y = _norm(x, p["final_w"], zero, zero, F32, tm).reshape(b, t, d)
    ev = [jnp.stack(s) for s in (k_list, v_list, mc_list, mn_list, mm_list)]
    od = [jnp.stack(s) for s in (conv_list, hl_list)]
    return y, ev, od


def kernel(x_prompt, x_sample, c_prompt, c_sample, cache_k, cache_v, state_mlstm_c, state_mlstm_n, state_mlstm_m, state_conv, state_rglru, norm_w, w_ada, b_ada, w_in_even, b_gates_even, lambda_q1, lambda_k1, lambda_q2, lambda_k2, subln_w, mh_norm_w, w_out_even, w_in_odd, conv_w, conv_b, rg_wa, rg_ba, rg_wx, rg_bx, rg_lambda, w_out_odd, final_w):
    p = dict(norm_w=norm_w, w_in_even=w_in_even, b_gates_even=b_gates_even,
             lambda_q1=lambda_q1, lambda_k1=lambda_k1, lambda_q2=lambda_q2, lambda_k2=lambda_k2,
             subln_w=subln_w, mh_norm_w=mh_norm_w, w_out_even=w_out_even, w_in_odd=w_in_odd,
             conv_w=conv_w, conv_b=conv_b, rg_wa=rg_wa, rg_ba=rg_ba, rg_wx=rg_wx, rg_bx=rg_bx,
             rg_lambda=rg_lambda, w_out_odd=w_out_odd, final_w=final_w)
    depth, d = norm_w.shape
    bp, bs = c_prompt.shape[0], c_sample.shape[0]
    rpad = -(bp + bs) % SUBLANES
    c_all = jnp.pad(jnp.concatenate([c_prompt, c_sample], axis=0), ((0, rpad), (0, 0)))
    mod = _modulation(c_all, w_ada, b_ada)

    def mods_for(lo, hi):
        return [tuple(mod[li, lo:hi, i * d:(i + 1) * d] for i in range(3)) for li in range(depth)]

    y_p, ev_p, od_p = _trunk(x_prompt, mods_for(0, bp), p, None, True)
    cache = dict(k=cache_k, v=cache_v, c=state_mlstm_c, n=state_mlstm_n, m=state_mlstm_m,
                 conv=state_conv, h=state_rglru)
    y_s, ev_s, od_s = _trunk(x_sample, mods_for(bp, bp + bs), p, cache, False)
    return (y_p, y_s, *ev_p, *od_p, *ev_s, *od_s)
```

```python
import functools
import math

import jax
import jax.numpy as jnp
from jax import lax
from jax.experimental import pallas as pl
from jax.experimental.pallas import tpu as pltpu

F32 = jnp.float32
BF16 = jnp.bfloat16
EPS = 1e-6
CHUNK = 64
C_RG = 8.0
CONV_W = 4
LANES = 128
SUBLANES = 8
ONES_ROWS = 16
NT_DIMS = (((1,), (1,)), ((), ()))
TN_DIMS = (((0,), (0,)), ((), ()))
HIGHEST = lax.Precision.HIGHEST
VMEM_LIMIT = 56 * 1024 * 1024


def _params(sem):
    return pltpu.CompilerParams(dimension_semantics=sem, vmem_limit_bytes=VMEM_LIMIT)


def _silu(z):
    return z * jax.nn.sigmoid(z)


def _mod_kernel(c_ref, w_ref, b_ref, o_ref):
    sc = _silu(c_ref[...]).astype(BF16)
    o_ref[0] = jnp.dot(sc, w_ref[0].astype(BF16), preferred_element_type=F32) + b_ref[0]


def _modulation(c_all, w_ada, b_ada):
    depth, d, n = w_ada.shape
    r = c_all.shape[0]
    tn = 1024
    return pl.pallas_call(
        _mod_kernel,
        grid=(depth, n // tn),
        in_specs=[
            pl.BlockSpec((r, d), lambda l, j: (0, 0)),
            pl.BlockSpec((1, d, tn), lambda l, j: (l, 0, j)),
            pl.BlockSpec((1, 1, tn), lambda l, j: (l, 0, j)),
        ],
        out_specs=pl.BlockSpec((1, r, tn), lambda l, j: (l, 0, j)),
        out_shape=jax.ShapeDtypeStruct((depth, r, n), F32),
        compiler_params=_params(("arbitrary", "arbitrary")),
        name="adaln_mod",
    )(c_all, w_ada, b_ada.reshape(depth, 1, n))


def _norm_kernel(x_ref, nw_ref, sc_ref, sh_ref, o_ref):
    x = x_ref[...]
    ms = jnp.mean(x * x, axis=-1, keepdims=True)
    y = (x * lax.rsqrt(ms + EPS)) * nw_ref[...]
    o_ref[...] = (y * (1.0 + sc_ref[...]) + sh_ref[...]).astype(o_ref.dtype)


def _row_spec(arr, tm):
    if arr.shape[0] == 1:
        return pl.BlockSpec((1, arr.shape[1]), lambda i: (0, 0))
    return pl.BlockSpec((tm, arr.shape[1]), lambda i: (i, 0))


def _norm(x, nw, scale, shift, out_dtype, tm):
    m, d = x.shape
    return pl.pallas_call(
        _norm_kernel,
        grid=(m // tm,),
        in_specs=[
            pl.BlockSpec((tm, d), lambda i: (i, 0)),
            pl.BlockSpec((1, d), lambda i: (0, 0)),
            _row_spec(scale, tm),
            _row_spec(shift, tm),
        ],
        out_specs=pl.BlockSpec((tm, d), lambda i: (i, 0)),
        out_shape=jax.ShapeDtypeStruct((m, d), out_dtype),
        compiler_params=_params(("arbitrary",)),
        name="adaln_norm",
    )(x, nw.reshape(1, d), scale, shift)


def _proj_acc(h_ref, w_ref, wbf_ref):
    @pl.when(pl.program_id(1) == 0)
    def _():
        wbf_ref[...] = w_ref[...].astype(BF16)

    return jnp.dot(h_ref[...], wbf_ref[...], preferred_element_type=F32)


def _proj_f32_kernel(h_ref, w_ref, o_ref, wbf_ref):
    o_ref[...] = _proj_acc(h_ref, w_ref, wbf_ref)


def _proj_bias_kernel(h_ref, w_ref, b_ref, o_ref, wbf_ref):
    o_ref[...] = _proj_acc(h_ref, w_ref, wbf_ref) + b_ref[...]


def _proj_bf16_kernel(h_ref, w_ref, o_ref, wbf_ref, *, scales):
    acc = _proj_acc(h_ref, w_ref, wbf_ref)
    j = pl.program_id(0)
    s = jnp.float32(scales[0])
    for t in range(1, len(scales)):
        s = jnp.where(j == t, jnp.float32(scales[t]), s)
    o_ref[...] = (acc * s).astype(BF16)


def _proj_k_kernel(h_ref, w_ref, o_ref, obf_ref, wbf_ref):
    acc = _proj_acc(h_ref, w_ref, wbf_ref)
    o_ref[...] = acc
    obf_ref[...] = acc.astype(BF16)


def _proj_vt_kernel(h_ref, w_ref, o_ref, ot_ref, wbf_ref, *, tk):
    acc = _proj_acc(h_ref, w_ref, wbf_ref)
    o_ref[...] = acc
    at = acc.T.astype(BF16)
    nh, nt = ot_ref.shape[0], ot_ref.shape[1]
    va = ot_ref.shape[2] - ONES_ROWS
    ones = (lax.broadcasted_iota(jnp.int32, (ONES_ROWS, tk), 0) == 0).astype(BF16)
    for hh in range(nh):
        for t in range(nt):
            ot_ref[hh, t, 0:va, :] = at[hh * va:(hh + 1) * va, t * tk:(t + 1) * tk]
            ot_ref[hh, t, va:va + ONES_ROWS, :] = ones


def _q_split(acc, axis, da, qk_scale):
    idx = lax.broadcasted_iota(jnp.int32, acc.shape, axis)
    first = (idx % (2 * da)) < da
    a = acc * qk_scale
    zero = jnp.zeros_like(a)
    return jnp.where(first, a, zero).astype(BF16), jnp.where(first, zero, a).astype(BF16)


def _proj_q_kernel(h_ref, w_ref, o1_ref, o2_ref, wbf_ref, *, da, qk_scale):
    acc = _proj_acc(h_ref, w_ref, wbf_ref)
    o1_ref[...], o2_ref[...] = _q_split(acc, 1, da, qk_scale)


def _proj_qt_kernel(h_ref, w_ref, o1_ref, o2_ref, wbf_ref, *, da, qk_scale):
    acc = _proj_acc(h_ref, w_ref, wbf_ref)
    o1_ref[...], o2_ref[...] = _q_split(acc.T, 0, da, qk_scale)


def _col_index(cols, j):
    idx = jnp.int32(cols[0])
    for t in range(1, len(cols)):
        idx = jnp.where(j == t, jnp.int32(cols[t]), idx)
    return idx


def _proj_call(body, h, w, cols, tn, tm, out_shapes, out_specs, extra_in=(), extra_specs=(), name="proj"):
    m, d = h.shape
    nj = len(cols)
    return pl.pallas_call(
        body,
        grid=(nj, m // tm),
        in_specs=[
            pl.BlockSpec((tm, d), lambda j, i: (i, 0)),
            pl.BlockSpec((d, tn), lambda j, i: (0, _col_index(cols, j))),
            *extra_specs,
        ],
        out_specs=out_specs,
        out_shape=out_shapes,
        scratch_shapes=[pltpu.VMEM((d, tn), BF16)],
        compiler_params=_params(("arbitrary", "arbitrary")),
        name=name,
    )(h, w, *extra_in)


def _proj_f32(h, w, cols, tn, tm):
    m = h.shape[0]
    return _proj_call(
        _proj_f32_kernel, h, w, cols, tn, tm,
        jax.ShapeDtypeStruct((len(cols), m, tn), F32),
        pl.BlockSpec((None, tm, tn), lambda j, i: (j, i, 0)), name="proj_f32")


def _proj_bf16(h, w, cols, scales, tn, tm):
    m = h.shape[0]
    return _proj_call(
        functools.partial(_proj_bf16_kernel, scales=tuple(scales)), h, w, cols, tn, tm,
        jax.ShapeDtypeStruct((len(cols), m, tn), BF16),
        pl.BlockSpec((None, tm, tn), lambda j, i: (j, i, 0)), name="proj_bf16")


def _proj_bias(h, w_small, bias, tm):
    m = h.shape[0]
    tn = w_small.shape[1]
    return _proj_call(
        _proj_bias_kernel, h, w_small, (0,), tn, tm,
        jax.ShapeDtypeStruct((m, tn), F32),
        pl.BlockSpec((tm, tn), lambda j, i: (i, 0)),
        extra_in=(bias,), extra_specs=(pl.BlockSpec((1, tn), lambda j, i: (0, 0)),), name="proj_gates")


def _proj_k(h, w, col, tn, tm):
    m = h.shape[0]
    return _proj_call(
        _proj_k_kernel, h, w, (col,), tn, tm,
        (jax.ShapeDtypeStruct((m, tn), F32), jax.ShapeDtypeStruct((m, tn), BF16)),
        (pl.BlockSpec((tm, tn), lambda j, i: (i, 0)), pl.BlockSpec((tm, tn), lambda j, i: (i, 0))),
        name="proj_k")


def _proj_vt(h, w, col, tn, tm, heads, tk):
    m = h.shape[0]
    va = tn // heads
    return _proj_call(
        functools.partial(_proj_vt_kernel, tk=tk), h, w, (col,), tn, tm,
        (jax.ShapeDtypeStruct((m, tn), F32),
         jax.ShapeDtypeStruct((heads, m // tk, va + ONES_ROWS, tk), BF16)),
        (pl.BlockSpec((tm, tn), lambda j, i: (i, 0)),
         pl.BlockSpec((heads, tm // tk, va + ONES_ROWS, tk), lambda j, i: (0, i, 0, 0))),
        name="proj_vt")


def _proj_q(h, w, col, tn, tm, da, transposed):
    m = h.shape[0]
    qk_scale = float(da) ** -0.5
    if transposed:
        qk_scale *= math.log2(math.e)
        body = functools.partial(_proj_qt_kernel, da=da, qk_scale=qk_scale)
        shape = jax.ShapeDtypeStruct((tn, m), BF16)
        spec = pl.BlockSpec((tn, tm), lambda j, i: (0, i))
    else:
        body = functools.partial(_proj_q_kernel, da=da, qk_scale=qk_scale)
        shape = jax.ShapeDtypeStruct((m, tn), BF16)
        spec = pl.BlockSpec((tm, tn), lambda j, i: (i, 0))
    return _proj_call(body, h, w, (col,), tn, tm, (shape, shape), (spec, spec), name="proj_q")


def _attn_finish(o, z, sw, post):
    ms = jnp.mean(o * o, axis=-1, keepdims=True)
    on = (o * lax.rsqrt(ms + EPS)) * sw
    return (on * post) * _silu(z)


def _attn_prompt_kernel(sc_ref, q1_ref, q2_ref, k_ref, vt_ref, za_ref, sw_ref, o_ref,
                        m_ref, acc_ref, sa_ref, sb_ref, *, tq, tk):
    i = pl.program_id(1)
    q_refs = (q1_ref, q2_ref)
    m_ref[...] = jnp.full(m_ref.shape, -jnp.inf, F32)
    acc_ref[...] = jnp.zeros(acc_ref.shape, F32)
    r = tq // tk
    va = o_ref.shape[1]

    def scores(j, c, lo):
        kt = k_ref[pl.ds(pl.multiple_of(j * tk, tk), tk), :]
        return jnp.dot(kt, q_refs[c][:, lo:], preferred_element_type=F32)

    def update(j, c, s, lo):
        vt = vt_ref[j]
        m_old = m_ref[c, :, lo:]
        m_new = jnp.maximum(m_old, jnp.max(s, axis=0, keepdims=True))
        alpha = jnp.exp2(m_old - m_new)
        p = jnp.exp2(s - m_new)
        acc_ref[c, :, lo:] = (alpha * acc_ref[c, :, lo:]
                              + jnp.dot(vt, p.astype(BF16), preferred_element_type=F32))
        m_ref[c, :, lo:] = m_new

    def fill(j, s_ref):
        for c in range(2):
            s_ref[c] = scores(j, c, 0)

    def drain(j, s_ref):
        for c in range(2):
            update(j, c, s_ref[c], 0)

    n = i * r

    @pl.when(n > 0)
    def _():
        fill(0, sa_ref)

        def body(t, carry):
            j = 2 * t
            fill(j + 1, sb_ref)
            drain(j, sa_ref)
            fill(j + 2, sa_ref)
            drain(j + 1, sb_ref)
            return carry

        lax.fori_loop(0, n // 2 - 1, body, 0)
        fill(n - 1, sb_ref)
        drain(n - 2, sa_ref)
        drain(n - 1, sb_ref)

    for dd in range(r):
        j = n + dd
        lo = dd * tk
        for c in range(2):
            s = scores(j, c, lo)
            kc = lax.broadcasted_iota(jnp.int32, s.shape, 0) // CHUNK
            qc = lax.broadcasted_iota(jnp.int32, s.shape, 1) // CHUNK
            update(j, c, jnp.where(kc <= qc, s, -jnp.inf), lo)

    lam = sc_ref[0, 0]
    post = sc_ref[0, 1]
    o = (acc_ref[0, 0:va, :] / acc_ref[0, va:va + 1, :]
         - lam * (acc_ref[1, 0:va, :] / acc_ref[1, va:va + 1, :]))
    o_ref[...] = _attn_finish(o.T, za_ref[...], sw_ref[...], post).astype(o_ref.dtype)


def _attn_prompt(scal, q1t, q2t, kbf, vt, fz, sw, heads, tq):
    dq, m = q1t.shape
    hd = dq // heads
    tk = vt.shape[3]
    hv = vt.shape[2]
    assert tq % (2 * tk) == 0 and m % tq == 0
    nk = m // tk
    return pl.pallas_call(
        functools.partial(_attn_prompt_kernel, tq=tq, tk=tk),
        grid=(heads, m // tq),
        in_specs=[
            pl.BlockSpec(memory_space=pltpu.SMEM),
            pl.BlockSpec((hd, tq), lambda h, i: (h, i)),
            pl.BlockSpec((hd, tq), lambda h, i: (h, i)),
            pl.BlockSpec((m, hd), lambda h, i: (0, h)),
            pl.BlockSpec((None, nk, hv, tk), lambda h, i: (h, 0, 0, 0)),
            pl.BlockSpec((None, tq, hd), lambda h, i: (0, i, h)),
            pl.BlockSpec((1, hd), lambda h, i: (0, 0)),
        ],
        out_specs=pl.BlockSpec((tq, hd), lambda h, i: (i, h)),
        out_shape=jax.ShapeDtypeStruct((m, dq), BF16),
        scratch_shapes=[
            pltpu.VMEM((2, 1, tq), F32),
            pltpu.VMEM((2, hv, tq), F32),
            pltpu.VMEM((2, tk, tq), F32),
            pltpu.VMEM((2, tk, tq), F32),
        ],
        compiler_params=_params(("arbitrary", "arbitrary")),
        name="diff_attn_prompt",
    )(scal, q1t, q2t, kbf, vt, fz, sw)


def _attn_sample_kernel(sc_ref, q1_ref, q2_ref, ck_ref, cv_ref, kn_ref, vn_ref, za_ref, sw_ref, o_ref):
    ck = ck_ref[...].astype(BF16)
    cv = cv_ref[...].astype(BF16)
    kn = kn_ref[...].astype(BF16)
    vn = vn_ref[...].astype(BF16)
    outs = []
    for q_ref in (q1_ref, q2_ref):
        q = q_ref[...]
        sc = lax.dot_general(q, ck, NT_DIMS, preferred_element_type=F32)
        sn = lax.dot_general(q, kn, NT_DIMS, preferred_element_type=F32)
        mx = jnp.maximum(jnp.max(sc, axis=-1, keepdims=True), jnp.max(sn, axis=-1, keepdims=True))
        pc = jnp.exp(sc - mx)
        pn = jnp.exp(sn - mx)
        l = jnp.sum(pc, axis=-1, keepdims=True) + jnp.sum(pn, axis=-1, keepdims=True)
        acc = (jnp.dot(pc.astype(BF16), cv, preferred_element_type=F32)
               + jnp.dot(pn.astype(BF16), vn, preferred_element_type=F32))
        outs.append(acc / l)
    lam = sc_ref[0, 0]
    post = sc_ref[0, 1]
    o = outs[0] - lam * outs[1]
    o_ref[...] = _attn_finish(o, za_ref[...], sw_ref[...], post).astype(o_ref.dtype)


def _attn_sample(scal, q1, q2, ck, cv, kv, fz, sw, heads, t):
    m, dq = q1.shape
    hd = dq // heads
    b, p, _ = ck.shape
    row = lambda bb, h: (bb, h)
    return pl.pallas_call(
        _attn_sample_kernel,
        grid=(b, heads),
        in_specs=[
            pl.BlockSpec(memory_space=pltpu.SMEM),
            pl.BlockSpec((t, hd), row),
            pl.BlockSpec((t, hd), row),
            pl.BlockSpec((None, p, hd), lambda bb, h: (bb, 0, h)),
            pl.BlockSpec((None, p, hd), lambda bb, h: (bb, 0, h)),
            pl.BlockSpec((None, t, hd), lambda bb, h: (0, bb, h)),
            pl.BlockSpec((None, t, hd), lambda bb, h: (1, bb, h)),
            pl.BlockSpec((None, t, hd), lambda bb, h: (0, bb, h)),
            pl.BlockSpec((1, hd), lambda bb, h: (0, 0)),
        ],
        out_specs=pl.BlockSpec((t, hd), row),
        out_shape=jax.ShapeDtypeStruct((m, dq), BF16),
        compiler_params=_params(("arbitrary", "arbitrary")),
        name="diff_attn_sample",
    )(scal, q1, q2, ck, cv, kv, kv, fz, sw)


def _mlstm_kernel(q_ref, k_ref, v_ref, ob_ref, zb_ref, g_ref, c0_ref, n0_ref, m0_ref, w_ref,
                  hb_ref, c_out, n_out, m_out, c_scr, n_scr, m_scr, *, L, cpb, H, DH):
    t = pl.program_id(1)

    @pl.when(t == 0)
    def _():
        c_scr[...] = c0_ref[0]
        n_scr[...] = n0_ref[0]
        m_scr[...] = m0_ref[0]

    row = lax.broadcasted_iota(jnp.int32, (L, L), 0)
    col = lax.broadcasted_iota(jnp.int32, (L, L), 1)
    causal = col <= row
    tril = causal.astype(F32)
    sel = (lax.broadcasted_iota(jnp.int32, (2 * H, LANES), 0)
           == lax.broadcasted_iota(jnp.int32, (2 * H, LANES), 1)).astype(F32)
    lane = lax.broadcasted_iota(jnp.int32, (1, LANES), 1)

    for cc in range(cpb):
        r0 = cc * L
        g = g_ref[r0:r0 + L, :]
        lf = jax.nn.log_sigmoid(g)
        bcum = jnp.dot(tril, lf, precision=HIGHEST, preferred_element_type=F32)
        x = jnp.where(lane < H, g, bcum)
        xt = lax.dot_general(sel, x, NT_DIMS, precision=HIGHEST, preferred_element_type=F32)
        m_all = m_scr[...]
        m_next = m_all
        for h in range(H):
            ig_c = g[:, h:h + 1]
            b_c = bcum[:, H + h:H + h + 1]
            ig_r = xt[h:h + 1, :]
            b_r = xt[H + h:H + h + 1, :]
            m0 = m_all[:, h:h + 1]
            dmat = jnp.where(causal, b_c - b_r + ig_r, -jnp.inf)
            m_t = jnp.maximum(b_c + m0, jnp.max(dmat, axis=-1, keepdims=True))
            wgt = jnp.exp(dmat - m_t)
            inter = jnp.exp(b_c + m0 - m_t)
            hs = slice(h * DH, (h + 1) * DH)
            qh = q_ref[r0:r0 + L, hs]
            kh = k_ref[r0:r0 + L, hs]
            vh = v_ref[r0:r0 + L, hs]
            s = lax.dot_general(qh, kh, NT_DIMS, preferred_element_type=F32) * wgt
            c_h = c_scr[h]
            n_h = n_scr[h:h + 1, :]
            num = (jnp.dot(s.astype(BF16), vh, preferred_element_type=F32)
                   + inter * jnp.dot(qh, c_h.astype(BF16), preferred_element_type=F32))
            den = (jnp.sum(s, axis=-1, keepdims=True)
                   + inter * jnp.sum(qh.astype(F32) * n_h, axis=-1, keepdims=True))
            denom = jnp.maximum(jnp.abs(den), jnp.exp(-m_t))
            hh = jax.nn.sigmoid(ob_ref[r0:r0 + L, hs]) * (num / denom)
            ms = jnp.mean(hh * hh, axis=-1, keepdims=True)
            hn = (hh * lax.rsqrt(ms + EPS)) * w_ref[:, hs]
            hb_ref[r0:r0 + L, hs] = (hn * _silu(zb_ref[r0:r0 + L, hs])).astype(hb_ref.dtype)
            b_last = b_c[L - 1:L, :]
            m_last = m_t[L - 1:L, :]
            ws = jnp.exp(b_last - b_c + ig_c - m_last)
            decay = jnp.exp(b_last + m0 - m_last)
            wv = (ws * vh.astype(F32)).astype(BF16)
            c_scr[h] = decay * c_h + lax.dot_general(kh, wv, TN_DIMS, preferred_element_type=F32)
            n_scr[h:h + 1, :] = decay * n_h + jnp.sum(ws * kh.astype(F32), axis=0, keepdims=True)
            m_next = jnp.where(lane == h, m_last, m_next)
        m_scr[...] = m_next

    @pl.when(t == pl.num_programs(1) - 1)
    def _():
        c_out[0] = c_scr[...]
        n_out[0] = n_scr[...]
        m_out[0] = m_scr[...]


def _mlstm(qkv, fz, gates, c0, n0, m0, mh_w, batch, L, cpb):
    _, m, wb = qkv.shape
    _, H, DH, _ = c0.shape
    rows = L * cpb
    nt = m // batch // rows
    blk = lambda which: pl.BlockSpec((None, rows, wb), lambda b, t: (which, b * nt + t, 0))
    return pl.pallas_call(
        functools.partial(_mlstm_kernel, L=L, cpb=cpb, H=H, DH=DH),
        grid=(batch, nt),
        in_specs=[
            blk(0), blk(1), blk(2),
            blk(1), blk(2),
            pl.BlockSpec((rows, LANES), lambda b, t: (b * nt + t, 0)),
            pl.BlockSpec((1, H, DH, DH), lambda b, t: (b, 0, 0, 0)),
            pl.BlockSpec((1, H, DH), lambda b, t: (b, 0, 0)),
            pl.BlockSpec((1, 1, LANES), lambda b, t: (b, 0, 0)),
            pl.BlockSpec((1, wb), lambda b, t: (0, 0)),
        ],
        out_specs=(
            pl.BlockSpec((rows, wb), lambda b, t: (b * nt + t, 0)),
            pl.BlockSpec((1, H, DH, DH), lambda b, t: (b, 0, 0, 0)),
            pl.BlockSpec((1, H, DH), lambda b, t: (b, 0, 0)),
            pl.BlockSpec((1, 1, LANES), lambda b, t: (b, 0, 0)),
        ),
        out_shape=(
            jax.ShapeDtypeStruct((m, wb), BF16),
            jax.ShapeDtypeStruct((batch, H, DH, DH), F32),
            jax.ShapeDtypeStruct((batch, H, DH), F32),
            jax.ShapeDtypeStruct((batch, 1, LANES), F32),
        ),
        scratch_shapes=[
            pltpu.VMEM((H, DH, DH), F32),
            pltpu.VMEM((H, DH), F32),
            pltpu.VMEM((1, LANES), F32),
        ],
        compiler_params=_params(("arbitrary", "arbitrary")),
        name="mlstm_chunks",
    )(qkv, qkv, qkv, fz, fz, gates, c0, n0, m0, mh_w)


def _outproj_kernel(*refs, n_in):
    a_refs = refs[:n_in]
    w_ref, x_ref, g_ref, o_ref, wbf_ref = refs[n_in:]

    @pl.when(pl.program_id(0) == 0)
    def _():
        wbf_ref[...] = w_ref[...].astype(BF16)

    acc = None
    off = 0
    for a_ref in a_refs:
        kk = a_ref.shape[1]
        part = jnp.dot(a_ref[...], wbf_ref[off:off + kk, :], preferred_element_type=F32)
        acc = part if acc is None else acc + part
        off += kk
    o_ref[...] = x_ref[...] + g_ref[...] * acc


def _outproj(acts, w, x, gate, tm):
    m, d = x.shape
    kt = w.shape[0]
    return pl.pallas_call(
        functools.partial(_outproj_kernel, n_in=len(acts)),
        grid=(m // tm,),
        in_specs=[
            *[pl.BlockSpec((tm, a.shape[1]), lambda i: (i, 0)) for a in acts],
            pl.BlockSpec((kt, d), lambda i: (0, 0)),
            pl.BlockSpec((tm, d), lambda i: (i, 0)),
            _row_spec(gate, tm),
        ],
        out_specs=pl.BlockSpec((tm, d), lambda i: (i, 0)),
        out_shape=jax.ShapeDtypeStruct((m, d), F32),
        scratch_shapes=[pltpu.VMEM((kt, d), BF16)],
        compiler_params=_params(("arbitrary",)),
        name="outproj_residual",
    )(*acts, w, x, gate)


def _rglru_kernel(xr_ref, zr_ref, cw_ref, cb_ref, wg_ref, ba_ref, bx_ref, lam_ref, buf0_ref, h0_ref,
                  y_ref, cs_ref, hl_ref, xbuf, a_scr, u_scr, hcar, *, tt, nblk, bw):
    t = pl.program_id(1)
    pad = SUBLANES

    @pl.when(t == 0)
    def _():
        xbuf[0:pad, :] = buf0_ref[0]
        hcar[...] = h0_ref[0]

    xbuf[pad:pad + tt, :] = xr_ref[...]
    xc = cb_ref[...] + cw_ref[CONV_W - 1:CONV_W, :] * xbuf[pad:pad + tt, :]
    for j in range(1, CONV_W):
        xc = xc + cw_ref[CONV_W - 1 - j:CONV_W - j, :] * xbuf[pad - j:pad - j + tt, :]

    neg_c = -C_RG * jax.nn.softplus(-lam_ref[...])
    for hb in range(nblk):
        cs = slice(hb * bw, (hb + 1) * bw)
        xb = xc[:, cs]
        gg = jnp.dot(xb.astype(BF16), wg_ref[hb], preferred_element_type=F32)
        r = jax.nn.sigmoid(gg[:, :bw] + ba_ref[:, cs])
        ig = jax.nn.sigmoid(gg[:, bw:] + bx_ref[:, cs])
        log_a = neg_c[:, cs] * r
        th = jnp.tanh(log_a)
        one_minus_a2 = (-2.0 * th) / (1.0 - th)
        a_scr[:, cs] = jnp.exp(log_a)
        u_scr[:, cs] = jnp.sqrt(one_minus_a2) * (ig * xb)

    rowi = lax.broadcasted_iota(jnp.int32, (SUBLANES, a_scr.shape[1]), 0)

    def group(gi, hc):
        r0 = pl.multiple_of(gi * SUBLANES, SUBLANES)
        a = a_scr[pl.ds(r0, SUBLANES), :]
        u = u_scr[pl.ds(r0, SUBLANES), :]
        for dd in (1, 2, 4):
            ok = rowi >= dd
            u = jnp.where(ok, a * pltpu.roll(u, dd, 0) + u, u)
            a = jnp.where(ok, a * pltpu.roll(a, dd, 0), a)
        hgrp = u + a * hc
        u_scr[pl.ds(r0, SUBLANES), :] = hgrp
        return hgrp[SUBLANES - 1:SUBLANES, :]

    hc = lax.fori_loop(0, tt // SUBLANES, group, hcar[...])
    hcar[...] = hc
    y_ref[...] = (u_scr[...] * _silu(zr_ref[...])).astype(y_ref.dtype)
    xbuf[0:pad, :] = xbuf[tt:tt + pad, :]

    @pl.when(t == pl.num_programs(1) - 1)
    def _():
        cs_ref[0] = xbuf[0:pad, :]
        hl_ref[0] = hc


def _rglru(xz, cw, cb, wg, ba, bx, lam, buf0, h0, batch, tt):
    _, m, dr = xz.shape
    nblk, bw, _ = wg.shape
    nt = m // batch // tt
    vec = lambda: pl.BlockSpec((1, dr), lambda b, t: (0, 0))
    return pl.pallas_call(
        functools.partial(_rglru_kernel, tt=tt, nblk=nblk, bw=bw),
        grid=(batch, nt),
        in_specs=[
            pl.BlockSpec((None, tt, dr), lambda b, t: (0, b * nt + t, 0)),
            pl.BlockSpec((None, tt, dr), lambda b, t: (1, b * nt + t, 0)),
            pl.BlockSpec((CONV_W, dr), lambda b, t: (0, 0)),
            vec(),
            pl.BlockSpec((nblk, bw, 2 * bw), lambda b, t: (0, 0, 0)),
            vec(), vec(), vec(),
            pl.BlockSpec((1, SUBLANES, dr), lambda b, t: (b, 0, 0)),
            pl.BlockSpec((1, 1, dr), lambda b, t: (b, 0, 0)),
        ],
        out_specs=(
            pl.BlockSpec((tt, dr), lambda b, t: (b * nt + t, 0)),
            pl.BlockSpec((1, SUBLANES, dr), lambda b, t: (b, 0, 0)),
            pl.BlockSpec((1, 1, dr), lambda b, t: (b, 0, 0)),
        ),
        out_shape=(
            jax.ShapeDtypeStruct((m, dr), BF16),
            jax.ShapeDtypeStruct((batch, SUBLANES, dr), F32),
            jax.ShapeDtypeStruct((batch, 1, dr), F32),
        ),
        scratch_shapes=[
            pltpu.VMEM((tt + SUBLANES, dr), F32),
            pltpu.VMEM((tt, dr), F32),
            pltpu.VMEM((tt, dr), F32),
            pltpu.VMEM((1, dr), F32),
        ],
        compiler_params=_params(("arbitrary", "arbitrary")),
        name="rglru_block",
    )(xz, xz, cw, cb, wg, ba, bx, lam, buf0, h0)


def _tile(m, pref):
    t = min(m, pref)
    assert m % t == 0
    return t


def _trunk(x3, mods, p, cache, prompt):
    b, t, d = x3.shape
    m = b * t
    depth = p["norm_w"].shape[0]
    x = x3.reshape(m, d)
    tm = _tile(m, 1024)
    va = p["subln_w"].shape[1]
    da = va // 2
    wb = p["mh_norm_w"].shape[1]
    tn = p["w_out_even"].shape[1] - wb
    assert tn == wb
    heads = tn // va
    hb_heads = p["b_gates_even"].shape[1] // 2
    tk = 256
    tq = _tile(m, 1024)
    k_list, v_list, mc_list, mn_list, mm_list, conv_list, hl_list = [], [], [], [], [], [], []

    def rows(v):
        if b == 1:
            return v
        return jnp.broadcast_to(v[:, None, :], (b, t, d)).reshape(m, d)

    for li in range(depth):
        shift, scale, gate = mods[li]
        h = _norm(x, p["norm_w"][li], rows(scale), rows(shift), BF16, tm)
        if li % 2 == 0:
            e = li // 2
            w_in = p["w_in_even"][e]
            lam_init = 0.8 - 0.6 * math.exp(-0.3 * li)
            lam = (jnp.exp(jnp.sum(p["lambda_q1"][e] * p["lambda_k1"][e]))
                   - jnp.exp(jnp.sum(p["lambda_q2"][e] * p["lambda_k2"][e])) + lam_init)
            scal = jnp.stack([lam, jnp.float32(1.0 - lam_init)]).reshape(1, 2).astype(F32)
            sw = p["subln_w"][e].reshape(1, -1)
            fz = _proj_f32(h, w_in, (3, 7, 8), tn, tm)
            qkv_b = _proj_bf16(h, w_in, (4, 5, 6), (1.0, float(LANES) ** -0.5, 1.0), tn, tm)
            ng = 2 * hb_heads
            w_g = jnp.pad(w_in[:, 9 * tn:9 * tn + ng], ((0, 0), (0, LANES - ng)))
            b_g = jnp.pad(p["b_gates_even"][e], (0, LANES - ng)).reshape(1, LANES)
            gates = _proj_bias(h, w_g, b_g, tm)
            if prompt:
                q1, q2 = _proj_q(h, w_in, 0, tn, tm, da, True)
                k_f, k_b = _proj_k(h, w_in, 1, tn, tm)
                v_f, v_t = _proj_vt(h, w_in, 2, tn, tm, heads, tk)
                oa = _attn_prompt(scal, q1, q2, k_b, v_t, fz, sw, heads, tq)
                dh = wb // hb_heads
                c0 = jnp.zeros((b, hb_heads, dh, dh), F32)
                n0 = jnp.zeros((b, hb_heads, dh), F32)
                m0 = jnp.zeros((b, 1, LANES), F32)
                L, cpb = CHUNK, 2
            else:
                q1, q2 = _proj_q(h, w_in, 0, tn, tm, da, False)
                kv = _proj_f32(h, w_in, (1, 2), tn, tm)
                k_f, v_f = kv[0], kv[1]
                ck = cache["k"][e].reshape(b, -1, heads * 2 * da)
                cv = cache["v"][e].reshape(b, -1, heads * 2 * da)
                oa = _attn_sample(scal, q1, q2, ck, cv, kv, fz, sw, heads, t)
                c0 = cache["c"][e]
                n0 = cache["n"][e]
                m0 = jnp.pad(cache["m"][e], ((0, 0), (0, LANES - hb_heads))).reshape(b, 1, LANES)
                L, cpb = t, 1
            hb, c_n, n_n, m_n = _mlstm(qkv_b, fz, gates, c0, n0, m0,
                                       p["mh_norm_w"][e].reshape(1, -1), b, L, cpb)
            k_list.append(k_f.reshape(b, t, heads, 2, da))
            v_list.append(v_f.reshape(b, t, heads, 2 * da))
            mc_list.append(c_n)
            mn_list.append(n_n)
            mm_list.append(m_n[:, 0, :hb_heads])
            x = _outproj([oa, hb], p["w_out_even"][e], x, rows(gate), tm)
        else:
            o = li // 2
            w_in = p["w_in_odd"][o]
            dr = w_in.shape[1] // 2
            xz = _proj_f32(h, w_in, (0, 1), dr, _tile(m, 512))
            wg = jnp.concatenate([p["rg_wa"][o], p["rg_wx"][o]], axis=-1).astype(BF16)
            if cache is None:
                buf0 = jnp.zeros((b, SUBLANES, dr), F32)
                h0 = jnp.zeros((b, 1, dr), F32)
            else:
                buf0 = jnp.pad(cache["conv"][o], ((0, 0), (SUBLANES - (CONV_W - 1), 0), (0, 0)))
                h0 = cache["h"][o].reshape(b, 1, dr)
            y, cs, hl = _rglru(xz, p["conv_w"][o], p["conv_b"][o].reshape(1, dr), wg,
                               p["rg_ba"][o].reshape(1, dr), p["rg_bx"][o].reshape(1, dr),
                               p["rg_lambda"][o].reshape(1, dr), buf0, h0, b, _tile(t, 256))
            conv_list.append(cs[:, SUBLANES - (CONV_W - 1):, :])
            hl_list.append(hl[:, 0, :])
            x = _outproj([y], p["w_out_odd"][o], x, rows(gate), tm)
    zero = jnp.zeros((1, d), F32)
    y = _norm(x, p["final_w"], zero, zero, F32, tm).reshape(b, t, d)
    ev = [jnp.stack(s) for s in (k_list, v_list, mc_list, mn_list, mm_list)]
    od = [jnp.stack(s) for s in (conv_list, hl_list)]
    return y, ev, od


def kernel(x_prompt, x_sample, c_prompt, c_sample, cache_k, cache_v, state_mlstm_c, state_mlstm_n, state_mlstm_m, state_conv, state_rglru, norm_w, w_ada, b_ada, w_in_even, b_gates_even, lambda_q1, lambda_k1, lambda_q2, lambda_k2, subln_w, mh_norm_w, w_out_even, w_in_odd, conv_w, conv_b, rg_wa, rg_ba, rg_wx, rg_bx, rg_lambda, w_out_odd, final_w):
    p = dict(norm_w=norm_w, w_in_even=w_in_even, b_gates_even=b_gates_even,
             lambda_q1=lambda_q1, lambda_k1=lambda_k1, lambda_q2=lambda_q2, lambda_k2=lambda_k2,
             subln_w=subln_w, mh_norm_w=mh_norm_w, w_out_even=w_out_even, w_in_odd=w_in_odd,
             conv_w=conv_w, conv_b=conv_b, rg_wa=rg_wa, rg_ba=rg_ba, rg_wx=rg_wx, rg_bx=rg_bx,
             rg_lambda=rg_lambda, w_out_odd=w_out_odd, final_w=final_w)
    depth, d = norm_w.shape
    bp, bs = c_prompt.shape[0], c_sample.shape[0]
    rpad = -(bp + bs) % SUBLANES
    c_all = jnp.pad(jnp.concatenate([c_prompt, c_sample], axis=0), ((0, rpad), (0, 0)))
    mod = _modulation(c_all, w_ada, b_ada)

    def mods_for(lo, hi):
        return [tuple(mod[li, lo:hi, i * d:(i + 1) * d] for i in range(3)) for li in range(depth)]

    y_p, ev_p, od_p = _trunk(x_prompt, mods_for(0, bp), p, None, True)
    cache = dict(k=cache_k, v=cache_v, c=state_mlstm_c, n=state_mlstm_n, m=state_mlstm_m,
                 conv=state_conv, h=state_rglru)
    y_s, ev_s, od_s = _trunk(x_sample, mods_for(bp, bp + bs), p, cache, False)
    return (y_p, y_s, *ev_p, *od_p, *ev_s, *od_s)
```

```python
import functools
import math

import jax
import jax.numpy as jnp
from jax import lax
from jax.experimental import pallas as pl
from jax.experimental.pallas import tpu as pltpu

F32 = jnp.float32
BF16 = jnp.bfloat16
EPS = 1e-6
CHUNK = 64
C_RG = 8.0
CONV_W = 4
LANES = 128
SUBLANES = 8
ONES_ROWS = 16
NT_DIMS = (((1,), (1,)), ((), ()))
TN_DIMS = (((0,), (0,)), ((), ()))
HIGHEST = lax.Precision.HIGHEST
VMEM_LIMIT = 56 * 1024 * 1024


def _params(sem):
    return pltpu.CompilerParams(dimension_semantics=sem, vmem_limit_bytes=VMEM_LIMIT)


def _silu(z):
    return z * jax.nn.sigmoid(z)


def _mod_kernel(c_ref, w_ref, b_ref, o_ref):
    sc = _silu(c_ref[...]).astype(BF16)
    o_ref[0] = jnp.dot(sc, w_ref[0].astype(BF16), preferred_element_type=F32) + b_ref[0]


def _modulation(c_all, w_ada, b_ada):
    depth, d, n = w_ada.shape
    r = c_all.shape[0]
    tn = 1024
    return pl.pallas_call(
        _mod_kernel,
        grid=(depth, n // tn),
        in_specs=[
            pl.BlockSpec((r, d), lambda l, j: (0, 0)),
            pl.BlockSpec((1, d, tn), lambda l, j: (l, 0, j)),
            pl.BlockSpec((1, 1, tn), lambda l, j: (l, 0, j)),
        ],
        out_specs=pl.BlockSpec((1, r, tn), lambda l, j: (l, 0, j)),
        out_shape=jax.ShapeDtypeStruct((depth, r, n), F32),
        compiler_params=_params(("arbitrary", "arbitrary")),
        name="adaln_mod",
    )(c_all, w_ada, b_ada.reshape(depth, 1, n))


def _norm_kernel(x_ref, nw_ref, sc_ref, sh_ref, o_ref):
    x = x_ref[...]
    ms = jnp.mean(x * x, axis=-1, keepdims=True)
    y = (x * lax.rsqrt(ms + EPS)) * nw_ref[...]
    o_ref[...] = (y * (1.0 + sc_ref[...]) + sh_ref[...]).astype(o_ref.dtype)


def _row_spec(arr, tm):
    if arr.shape[0] == 1:
        return pl.BlockSpec((1, arr.shape[1]), lambda i: (0, 0))
    return pl.BlockSpec((tm, arr.shape[1]), lambda i: (i, 0))


def _norm(x, nw, scale, shift, out_dtype, tm):
    m, d = x.shape
    return pl.pallas_call(
        _norm_kernel,
        grid=(m // tm,),
        in_specs=[
            pl.BlockSpec((tm, d), lambda i: (i, 0)),
            pl.BlockSpec((1, d), lambda i: (0, 0)),
            _row_spec(scale, tm),
            _row_spec(shift, tm),
        ],
        out_specs=pl.BlockSpec((tm, d), lambda i: (i, 0)),
        out_shape=jax.ShapeDtypeStruct((m, d), out_dtype),
        compiler_params=_params(("arbitrary",)),
        name="adaln_norm",
    )(x, nw.reshape(1, d), scale, shift)


def _proj_acc(h_ref, w_ref, wbf_ref):
    @pl.when(pl.program_id(1) == 0)
    def _():
        wbf_ref[...] = w_ref[...].astype(BF16)

    return jnp.dot(h_ref[...], wbf_ref[...], preferred_element_type=F32)


def _proj_f32_kernel(h_ref, w_ref, o_ref, wbf_ref):
    o_ref[...] = _proj_acc(h_ref, w_ref, wbf_ref)


def _proj_bias_kernel(h_ref, w_ref, b_ref, o_ref, wbf_ref):
    o_ref[...] = _proj_acc(h_ref, w_ref, wbf_ref) + b_ref[...]


def _proj_bf16_kernel(h_ref, w_ref, o_ref, wbf_ref, *, scales):
    acc = _proj_acc(h_ref, w_ref, wbf_ref)
    j = pl.program_id(0)
    s = jnp.float32(scales[0])
    for t in range(1, len(scales)):
        s = jnp.where(j == t, jnp.float32(scales[t]), s)
    o_ref[...] = (acc * s).astype(BF16)


def _proj_k_kernel(h_ref, w_ref, o_ref, obf_ref, wbf_ref):
    acc = _proj_acc(h_ref, w_ref, wbf_ref)
    o_ref[...] = acc
    obf_ref[...] = acc.astype(BF16)


def _proj_vt_kernel(h_ref, w_ref, o_ref, ot_ref, wbf_ref, *, tk):
    acc = _proj_acc(h_ref, w_ref, wbf_ref)
    o_ref[...] = acc
    at = acc.T.astype(BF16)
    nh, nt = ot_ref.shape[0], ot_ref.shape[1]
    va = ot_ref.shape[2] - ONES_ROWS
    ones = (lax.broadcasted_iota(jnp.int32, (ONES_ROWS, tk), 0) == 0).astype(BF16)
    for hh in range(nh):
        for t in range(nt):
            ot_ref[hh, t, 0:va, :] = at[hh * va:(hh + 1) * va, t * tk:(t + 1) * tk]
            ot_ref[hh, t, va:va + ONES_ROWS, :] = ones


def _q_split(acc, axis, da, qk_scale):
    idx = lax.broadcasted_iota(jnp.int32, acc.shape, axis)
    first = (idx % (2 * da)) < da
    a = acc * qk_scale
    zero = jnp.zeros_like(a)
    return jnp.where(first, a, zero).astype(BF16), jnp.where(first, zero, a).astype(BF16)


def _proj_q_kernel(h_ref, w_ref, o1_ref, o2_ref, wbf_ref, *, da, qk_scale):
    acc = _proj_acc(h_ref, w_ref, wbf_ref)
    o1_ref[...], o2_ref[...] = _q_split(acc, 1, da, qk_scale)


def _proj_qt_kernel(h_ref, w_ref, o1_ref, o2_ref, wbf_ref, *, da, qk_scale):
    acc = _proj_acc(h_ref, w_ref, wbf_ref)
    o1_ref[...], o2_ref[...] = _q_split(acc.T, 0, da, qk_scale)


def _col_index(cols, j):
    idx = jnp.int32(cols[0])
    for t in range(1, len(cols)):
        idx = jnp.where(j == t, jnp.int32(cols[t]), idx)
    return idx


def _proj_call(body, h, w, cols, tn, tm, out_shapes, out_specs, extra_in=(), extra_specs=(), name="proj"):
    m, d = h.shape
    nj = len(cols)
    return pl.pallas_call(
        body,
        grid=(nj, m // tm),
        in_specs=[
            pl.BlockSpec((tm, d), lambda j, i: (i, 0)),
            pl.BlockSpec((d, tn), lambda j, i: (0, _col_index(cols, j))),
            *extra_specs,
        ],
        out_specs=out_specs,
        out_shape=out_shapes,
        scratch_shapes=[pltpu.VMEM((d, tn), BF16)],
        compiler_params=_params(("arbitrary", "arbitrary")),
        name=name,
    )(h, w, *extra_in)


def _proj_f32(h, w, cols, tn, tm):
    m = h.shape[0]
    return _proj_call(
        _proj_f32_kernel, h, w, cols, tn, tm,
        jax.ShapeDtypeStruct((len(cols), m, tn), F32),
        pl.BlockSpec((None, tm, tn), lambda j, i: (j, i, 0)), name="proj_f32")


def _proj_bf16(h, w, cols, scales, tn, tm):
    m = h.shape[0]
    return _proj_call(
        functools.partial(_proj_bf16_kernel, scales=tuple(scales)), h, w, cols, tn, tm,
        jax.ShapeDtypeStruct((len(cols), m, tn), BF16),
        pl.BlockSpec((None, tm, tn), lambda j, i: (j, i, 0)), name="proj_bf16")


def _proj_bias(h, w_small, bias, tm):
    m = h.shape[0]
    tn = w_small.shape[1]
    return _proj_call(
        _proj_bias_kernel, h, w_small, (0,), tn, tm,
        jax.ShapeDtypeStruct((m, tn), F32),
        pl.BlockSpec((tm, tn), lambda j, i: (i, 0)),
        extra_in=(bias,), extra_specs=(pl.BlockSpec((1, tn), lambda j, i: (0, 0)),), name="proj_gates")


def _proj_k(h, w, col, tn, tm):
    m = h.shape[0]
    return _proj_call(
        _proj_k_kernel, h, w, (col,), tn, tm,
        (jax.ShapeDtypeStruct((m, tn), F32), jax.ShapeDtypeStruct((m, tn), BF16)),
        (pl.BlockSpec((tm, tn), lambda j, i: (i, 0)), pl.BlockSpec((tm, tn), lambda j, i: (i, 0))),
        name="proj_k")


def _proj_vt(h, w, col, tn, tm, heads, tk):
    m = h.shape[0]
    va = tn // heads
    return _proj_call(
        functools.partial(_proj_vt_kernel, tk=tk), h, w, (col,), tn, tm,
        (jax.ShapeDtypeStruct((m, tn), F32),
         jax.ShapeDtypeStruct((heads, m // tk, va + ONES_ROWS, tk), BF16)),
        (pl.BlockSpec((tm, tn), lambda j, i: (i, 0)),
         pl.BlockSpec((heads, tm // tk, va + ONES_ROWS, tk), lambda j, i: (0, i, 0, 0))),
        name="proj_vt")


def _proj_q(h, w, col, tn, tm, da, transposed):
    m = h.shape[0]
    qk_scale = float(da) ** -0.5
    if transposed:
        qk_scale *= math.log2(math.e)
        body = functools.partial(_proj_qt_kernel, da=da, qk_scale=qk_scale)
        shape = jax.ShapeDtypeStruct((tn, m), BF16)
        spec = pl.BlockSpec((tn, tm), lambda j, i: (0, i))
    else:
        body = functools.partial(_proj_q_kernel, da=da, qk_scale=qk_scale)
        shape = jax.ShapeDtypeStruct((m, tn), BF16)
        spec = pl.BlockSpec((tm, tn), lambda j, i: (i, 0))
    return _proj_call(body, h, w, (col,), tn, tm, (shape, shape), (spec, spec), name="proj_q")


def _attn_finish(o, z, sw, post):
    ms = jnp.mean(o * o, axis=-1, keepdims=True)
    on = (o * lax.rsqrt(ms + EPS)) * sw
    return (on * post) * _silu(z)


def _attn_prompt_kernel(sc_ref, q1_ref, q2_ref, k_ref, vt_ref, za_ref, sw_ref, o_ref,
                        m_ref, acc_ref, sa_ref, sb_ref, cma_ref, cmb_ref, *, tq, tk):
    i = pl.program_id(1)
    q_refs = (q1_ref, q2_ref)
    m_ref[...] = jnp.full(m_ref.shape, -jnp.inf, F32)
    acc_ref[...] = jnp.zeros(acc_ref.shape, F32)
    r = tq // tk
    va = o_ref.shape[1]

    def scores(j, c, lo):
        kt = k_ref[pl.ds(pl.multiple_of(j * tk, tk), tk), :]
        return jnp.dot(kt, q_refs[c][:, lo:], preferred_element_type=F32)

    def update(j, c, s, smax, lo):
        vt = vt_ref[j]
        m_old = m_ref[c, :, lo:]
        m_new = jnp.maximum(m_old, smax)
        alpha = jnp.exp2(m_old - m_new)
        p = jnp.exp2(s - m_new)
        acc_ref[c, :, lo:] = (alpha * acc_ref[c, :, lo:]
                              + jnp.dot(vt, p.astype(BF16), preferred_element_type=F32))
        m_ref[c, :, lo:] = m_new

    def fill(j, buf, lo=0):
        s_ref, cm_ref = buf
        for c in range(2):
            s = scores(j, c, lo)
            s_ref[c, :, lo:] = s
            cm_ref[c, :, lo:] = jnp.max(s, axis=0, keepdims=True)

    def drain(j, buf, lo=0, masked=False):
        s_ref, cm_ref = buf
        for c in range(2):
            s = s_ref[c, :, lo:]
            if masked:
                kc = lax.broadcasted_iota(jnp.int32, s.shape, 0) // CHUNK
                qc = lax.broadcasted_iota(jnp.int32, s.shape, 1) // CHUNK
                s = jnp.where(kc <= qc, s, -jnp.inf)
                smax = jnp.max(s, axis=0, keepdims=True)
            else:
                smax = cm_ref[c, :, lo:]
            update(j, c, s, smax, lo)

    n = i * r
    bufs = ((sa_ref, cma_ref), (sb_ref, cmb_ref))
    fill(0, bufs[0])

    def body(t, carry):
        j = 2 * t
        fill(j + 1, bufs[1])
        drain(j, bufs[0])
        fill(j + 2, bufs[0])
        drain(j + 1, bufs[1])
        return carry

    lax.fori_loop(0, n // 2, body, 0)
    for dd in range(r):
        if dd + 1 < r:
            fill(n + dd + 1, bufs[(dd + 1) % 2], (dd + 1) * tk)
        drain(n + dd, bufs[dd % 2], dd * tk, masked=True)

    lam = sc_ref[0, 0]
    post = sc_ref[0, 1]
    o = (acc_ref[0, 0:va, :] / acc_ref[0, va:va + 1, :]
         - lam * (acc_ref[1, 0:va, :] / acc_ref[1, va:va + 1, :]))
    o_ref[...] = _attn_finish(o.T, za_ref[...], sw_ref[...], post).astype(o_ref.dtype)


def _attn_prompt(scal, q1t, q2t, kbf, vt, fz, sw, heads, tq):
    dq, m = q1t.shape
    hd = dq // heads
    tk = vt.shape[3]
    hv = vt.shape[2]
    assert tq % (2 * tk) == 0 and m % tq == 0
    nk = m // tk
    return pl.pallas_call(
        functools.partial(_attn_prompt_kernel, tq=tq, tk=tk),
        grid=(heads, m // tq),
        in_specs=[
            pl.BlockSpec(memory_space=pltpu.SMEM),
            pl.BlockSpec((hd, tq), lambda h, i: (h, i)),
            pl.BlockSpec((hd, tq), lambda h, i: (h, i)),
            pl.BlockSpec((m, hd), lambda h, i: (0, h)),
            pl.BlockSpec((None, nk, hv, tk), lambda h, i: (h, 0, 0, 0)),
            pl.BlockSpec((None, tq, hd), lambda h, i: (0, i, h)),
            pl.BlockSpec((1, hd), lambda h, i: (0, 0)),
        ],
        out_specs=pl.BlockSpec((tq, hd), lambda h, i: (i, h)),
        out_shape=jax.ShapeDtypeStruct((m, dq), BF16),
        scratch_shapes=[
            pltpu.VMEM((2, 1, tq), F32),
            pltpu.VMEM((2, hv, tq), F32),
            pltpu.VMEM((2, tk, tq), F32),
            pltpu.VMEM((2, tk, tq), F32),
            pltpu.VMEM((2, 1, tq), F32),
            pltpu.VMEM((2, 1, tq), F32),
        ],
        compiler_params=_params(("arbitrary", "arbitrary")),
        name="diff_attn_prompt",
    )(scal, q1t, q2t, kbf, vt, fz, sw)


def _attn_sample_kernel(sc_ref, q1_ref, q2_ref, ck_ref, cv_ref, kn_ref, vn_ref, za_ref, sw_ref, o_ref):
    ck = ck_ref[...].astype(BF16)
    cv = cv_ref[...].astype(BF16)
    kn = kn_ref[...].astype(BF16)
    vn = vn_ref[...].astype(BF16)
    outs = []
    for q_ref in (q1_ref, q2_ref):
        q = q_ref[...]
        sc = lax.dot_general(q, ck, NT_DIMS, preferred_element_type=F32)
        sn = lax.dot_general(q, kn, NT_DIMS, preferred_element_type=F32)
        mx = jnp.maximum(jnp.max(sc, axis=-1, keepdims=True), jnp.max(sn, axis=-1, keepdims=True))
        pc = jnp.exp(sc - mx)
        pn = jnp.exp(sn - mx)
        l = jnp.sum(pc, axis=-1, keepdims=True) + jnp.sum(pn, axis=-1, keepdims=True)
        acc = (jnp.dot(pc.astype(BF16), cv, preferred_element_type=F32)
               + jnp.dot(pn.astype(BF16), vn, preferred_element_type=F32))
        outs.append(acc / l)
    lam = sc_ref[0, 0]
    post = sc_ref[0, 1]
    o = outs[0] - lam * outs[1]
    o_ref[...] = _attn_finish(o, za_ref[...], sw_ref[...], post).astype(o_ref.dtype)


def _attn_sample(scal, q1, q2, ck, cv, kv, fz, sw, heads, t):
    m, dq = q1.shape
    hd = dq // heads
    b, p, _ = ck.shape
    row = lambda bb, h: (bb, h)
    return pl.pallas_call(
        _attn_sample_kernel,
        grid=(b, heads),
        in_specs=[
            pl.BlockSpec(memory_space=pltpu.SMEM),
            pl.BlockSpec((t, hd), row),
            pl.BlockSpec((t, hd), row),
            pl.BlockSpec((None, p, hd), lambda bb, h: (bb, 0, h)),
            pl.BlockSpec((None, p, hd), lambda bb, h: (bb, 0, h)),
            pl.BlockSpec((None, t, hd), lambda bb, h: (0, bb, h)),
            pl.BlockSpec((None, t, hd), lambda bb, h: (1, bb, h)),
            pl.BlockSpec((None, t, hd), lambda bb, h: (0, bb, h)),
            pl.BlockSpec((1, hd), lambda bb, h: (0, 0)),
        ],
        out_specs=pl.BlockSpec((t, hd), row),
        out_shape=jax.ShapeDtypeStruct((m, dq), BF16),
        compiler_params=_params(("arbitrary", "arbitrary")),
        name="diff_attn_sample",
    )(scal, q1, q2, ck, cv, kv, kv, fz, sw)


def _mlstm_kernel(q_ref, k_ref, v_ref, ob_ref, zb_ref, g_ref, c0_ref, n0_ref, m0_ref, w_ref,
                  hb_ref, c_out, n_out, m_out, c_scr, n_scr, m_scr, *, L, cpb, H, DH):
    t = pl.program_id(1)

    @pl.when(t == 0)
    def _():
        c_scr[...] = c0_ref[0]
        n_scr[...] = n0_ref[0]
        m_scr[...] = m0_ref[0]

    row = lax.broadcasted_iota(jnp.int32, (L, L), 0)
    col = lax.broadcasted_iota(jnp.int32, (L, L), 1)
    causal = col <= row
    tril = causal.astype(F32)
    sel = (lax.broadcasted_iota(jnp.int32, (2 * H, LANES), 0)
           == lax.broadcasted_iota(jnp.int32, (2 * H, LANES), 1)).astype(F32)
    lane = lax.broadcasted_iota(jnp.int32, (1, LANES), 1)

    for cc in range(cpb):
        r0 = cc * L
        g = g_ref[r0:r0 + L, :]
        lf = jax.nn.log_sigmoid(g)
        bcum = jnp.dot(tril, lf, precision=HIGHEST, preferred_element_type=F32)
        x = jnp.where(lane < H, g, bcum)
        xt = lax.dot_general(sel, x, NT_DIMS, precision=HIGHEST, preferred_element_type=F32)
        m_all = m_scr[...]
        m_next = m_all
        for h in range(H):
            ig_c = g[:, h:h + 1]
            b_c = bcum[:, H + h:H + h + 1]
            ig_r = xt[h:h + 1, :]
            b_r = xt[H + h:H + h + 1, :]
            m0 = m_all[:, h:h + 1]
            dmat = jnp.where(causal, b_c - b_r + ig_r, -jnp.inf)
            m_t = jnp.maximum(b_c + m0, jnp.max(dmat, axis=-1, keepdims=True))
            wgt = jnp.exp(dmat - m_t)
            inter = jnp.exp(b_c + m0 - m_t)
            hs = slice(h * DH, (h + 1) * DH)
            qh = q_ref[r0:r0 + L, hs]
            kh = k_ref[r0:r0 + L, hs]
            vh = v_ref[r0:r0 + L, hs]
            s = lax.dot_general(qh, kh, NT_DIMS, preferred_element_type=F32) * wgt
            c_h = c_scr[h]
            n_h = n_scr[h:h + 1, :]
            num = (jnp.dot(s.astype(BF16), vh, preferred_element_type=F32)
                   + inter * jnp.dot(qh, c_h.astype(BF16), preferred_element_type=F32))
            den = (jnp.sum(s, axis=-1, keepdims=True)
                   + inter * jnp.sum(qh.astype(F32) * n_h, axis=-1, keepdims=True))
            denom = jnp.maximum(jnp.abs(den), jnp.exp(-m_t))
            hh = jax.nn.sigmoid(ob_ref[r0:r0 + L, hs]) * (num / denom)
            ms = jnp.mean(hh * hh, axis=-1, keepdims=True)
            hn = (hh * lax.rsqrt(ms + EPS)) * w_ref[:, hs]
            hb_ref[r0:r0 + L, hs] = (hn * _silu(zb_ref[r0:r0 + L, hs])).astype(hb_ref.dtype)
            b_last = b_c[L - 1:L, :]
            m_last = m_t[L - 1:L, :]
            ws = jnp.exp(b_last - b_c + ig_c - m_last)
            decay = jnp.exp(b_last + m0 - m_last)
            wv = (ws * vh.astype(F32)).astype(BF16)
            c_scr[h] = decay * c_h + lax.dot_general(kh, wv, TN_DIMS, preferred_element_type=F32)
            n_scr[h:h + 1, :] = decay * n_h + jnp.sum(ws * kh.astype(F32), axis=0, keepdims=True)
            m_next = jnp.where(lane == h, m_last, m_next)
        m_scr[...] = m_next

    @pl.when(t == pl.num_programs(1) - 1)
    def _():
        c_out[0] = c_scr[...]
        n_out[0] = n_scr[...]
        m_out[0] = m_scr[...]


def _mlstm(qkv, fz, gates, c0, n0, m0, mh_w, batch, L, cpb):
    _, m, wb = qkv.shape
    _, H, DH, _ = c0.shape
    rows = L * cpb
    nt = m // batch // rows
    blk = lambda which: pl.BlockSpec((None, rows, wb), lambda b, t: (which, b * nt + t, 0))
    return pl.pallas_call(
        functools.partial(_mlstm_kernel, L=L, cpb=cpb, H=H, DH=DH),
        grid=(batch, nt),
        in_specs=[
            blk(0), blk(1), blk(2),
            blk(1), blk(2),
            pl.BlockSpec((rows, LANES), lambda b, t: (b * nt + t, 0)),
            pl.BlockSpec((1, H, DH, DH), lambda b, t: (b, 0, 0, 0)),
            pl.BlockSpec((1, H, DH), lambda b, t: (b, 0, 0)),
            pl.BlockSpec((1, 1, LANES), lambda b, t: (b, 0, 0)),
            pl.BlockSpec((1, wb), lambda b, t: (0, 0)),
        ],
        out_specs=(
            pl.BlockSpec((rows, wb), lambda b, t: (b * nt + t, 0)),
            pl.BlockSpec((1, H, DH, DH), lambda b, t: (b, 0, 0, 0)),
            pl.BlockSpec((1, H, DH), lambda b, t: (b, 0, 0)),
            pl.BlockSpec((1, 1, LANES), lambda b, t: (b, 0, 0)),
        ),
        out_shape=(
            jax.ShapeDtypeStruct((m, wb), BF16),
            jax.ShapeDtypeStruct((batch, H, DH, DH), F32),
            jax.ShapeDtypeStruct((batch, H, DH), F32),
            jax.ShapeDtypeStruct((batch, 1, LANES), F32),
        ),
        scratch_shapes=[
            pltpu.VMEM((H, DH, DH), F32),
            pltpu.VMEM((H, DH), F32),
            pltpu.VMEM((1, LANES), F32),
        ],
        compiler_params=_params(("arbitrary", "arbitrary")),
        name="mlstm_chunks",
    )(qkv, qkv, qkv, fz, fz, gates, c0, n0, m0, mh_w)


def _outproj_kernel(*refs, n_in):
    a_refs = refs[:n_in]
    w_ref, x_ref, g_ref, o_ref, wbf_ref = refs[n_in:]

    @pl.when(pl.program_id(0) == 0)
    def _():
        wbf_ref[...] = w_ref[...].astype(BF16)

    acc = None
    off = 0
    for a_ref in a_refs:
        kk = a_ref.shape[1]
        part = jnp.dot(a_ref[...], wbf_ref[off:off + kk, :], preferred_element_type=F32)
        acc = part if acc is None else acc + part
        off += kk
    o_ref[...] = x_ref[...] + g_ref[...] * acc


def _outproj(acts, w, x, gate, tm):
    m, d = x.shape
    kt = w.shape[0]
    return pl.pallas_call(
        functools.partial(_outproj_kernel, n_in=len(acts)),
        grid=(m // tm,),
        in_specs=[
            *[pl.BlockSpec((tm, a.shape[1]), lambda i: (i, 0)) for a in acts],
            pl.BlockSpec((kt, d), lambda i: (0, 0)),
            pl.BlockSpec((tm, d), lambda i: (i, 0)),
            _row_spec(gate, tm),
        ],
        out_specs=pl.BlockSpec((tm, d), lambda i: (i, 0)),
        out_shape=jax.ShapeDtypeStruct((m, d), F32),
        scratch_shapes=[pltpu.VMEM((kt, d), BF16)],
        compiler_params=_params(("arbitrary",)),
        name="outproj_residual",
    )(*acts, w, x, gate)


def _rglru_kernel(xr_ref, zr_ref, cw_ref, cb_ref, wg_ref, ba_ref, bx_ref, lam_ref, buf0_ref, h0_ref,
                  y_ref, cs_ref, hl_ref, xbuf, a_scr, u_scr, hcar, *, tt, nblk, bw):
    t = pl.program_id(1)
    pad = SUBLANES

    @pl.when(t == 0)
    def _():
        xbuf[0:pad, :] = buf0_ref[0]
        hcar[...] = h0_ref[0]

    xbuf[pad:pad + tt, :] = xr_ref[...]
    xc = cb_ref[...] + cw_ref[CONV_W - 1:CONV_W, :] * xbuf[pad:pad + tt, :]
    for j in range(1, CONV_W):
        xc = xc + cw_ref[CONV_W - 1 - j:CONV_W - j, :] * xbuf[pad - j:pad - j + tt, :]

    neg_c = -C_RG * jax.nn.softplus(-lam_ref[...])
    for hb in range(nblk):
        cs = slice(hb * bw, (hb + 1) * bw)
        xb = xc[:, cs]
        gg = jnp.dot(xb.astype(BF16), wg_ref[hb], preferred_element_type=F32)
        r = jax.nn.sigmoid(gg[:, :bw] + ba_ref[:, cs])
        ig = jax.nn.sigmoid(gg[:, bw:] + bx_ref[:, cs])
        log_a = neg_c[:, cs] * r
        th = jnp.tanh(log_a)
        one_minus_a2 = (-2.0 * th) / (1.0 - th)
        a_scr[:, cs] = jnp.exp(log_a)
        u_scr[:, cs] = jnp.sqrt(one_minus_a2) * (ig * xb)

    rowi = lax.broadcasted_iota(jnp.int32, (SUBLANES, a_scr.shape[1]), 0)

    def group(gi, hc):
        r0 = pl.multiple_of(gi * SUBLANES, SUBLANES)
        a = a_scr[pl.ds(r0, SUBLANES), :]
        u = u_scr[pl.ds(r0, SUBLANES), :]
        for dd in (1, 2, 4):
            ok = rowi >= dd
            u = jnp.where(ok, a * pltpu.roll(u, dd, 0) + u, u)
            a = jnp.where(ok, a * pltpu.roll(a, dd, 0), a)
        hgrp = u + a * hc
        u_scr[pl.ds(r0, SUBLANES), :] = hgrp
        return hgrp[SUBLANES - 1:SUBLANES, :]

    hc = lax.fori_loop(0, tt // SUBLANES, group, hcar[...])
    hcar[...] = hc
    y_ref[...] = (u_scr[...] * _silu(zr_ref[...])).astype(y_ref.dtype)
    xbuf[0:pad, :] = xbuf[tt:tt + pad, :]

    @pl.when(t == pl.num_programs(1) - 1)
    def _():
        cs_ref[0] = xbuf[0:pad, :]
        hl_ref[0] = hc


def _rglru(xz, cw, cb, wg, ba, bx, lam, buf0, h0, batch, tt):
    _, m, dr = xz.shape
    nblk, bw, _ = wg.shape
    nt = m // batch // tt
    vec = lambda: pl.BlockSpec((1, dr), lambda b, t: (0, 0))
    return pl.pallas_call(
        functools.partial(_rglru_kernel, tt=tt, nblk=nblk, bw=bw),
        grid=(batch, nt),
        in_specs=[
            pl.BlockSpec((None, tt, dr), lambda b, t: (0, b * nt + t, 0)),
            pl.BlockSpec((None, tt, dr), lambda b, t: (1, b * nt + t, 0)),
            pl.BlockSpec((CONV_W, dr), lambda b, t: (0, 0)),
            vec(),
            pl.BlockSpec((nblk, bw, 2 * bw), lambda b, t: (0, 0, 0)),
            vec(), vec(), vec(),
            pl.BlockSpec((1, SUBLANES, dr), lambda b, t: (b, 0, 0)),
            pl.BlockSpec((1, 1, dr), lambda b, t: (b, 0, 0)),
        ],
        out_specs=(
            pl.BlockSpec((tt, dr), lambda b, t: (b * nt + t, 0)),
            pl.BlockSpec((1, SUBLANES, dr), lambda b, t: (b, 0, 0)),
            pl.BlockSpec((1, 1, dr), lambda b, t: (b, 0, 0)),
        ),
        out_shape=(
            jax.ShapeDtypeStruct((m, dr), BF16),
            jax.ShapeDtypeStruct((batch, SUBLANES, dr), F32),
            jax.ShapeDtypeStruct((batch, 1, dr), F32),
        ),
        scratch_shapes=[
            pltpu.VMEM((tt + SUBLANES, dr), F32),
            pltpu.VMEM((tt, dr), F32),
            pltpu.VMEM((tt, dr), F32),
            pltpu.VMEM((1, dr), F32),
        ],
        compiler_params=_params(("arbitrary", "arbitrary")),
        name="rglru_block",
    )(xz, xz, cw, cb, wg, ba, bx, lam, buf0, h0)


def _tile(m, pref):
    t = min(m, pref)
    assert m % t == 0
    return t


def _trunk(x3, mods, p, cache, prompt):
    b, t, d = x3.shape
    m = b * t
    depth = p["norm_w"].shape[0]
    x = x3.reshape(m, d)
    tm = _tile(m, 1024)
    va = p["subln_w"].shape[1]
    da = va // 2
    wb = p["mh_norm_w"].shape[1]
    tn = p["w_out_even"].shape[1] - wb
    assert tn == wb
    heads = tn // va
    hb_heads = p["b_gates_even"].shape[1] // 2
    tk = 512
    tq = _tile(m, 1024)
    k_list, v_list, mc_list, mn_list, mm_list, conv_list, hl_list = [], [], [], [], [], [], []

    def rows(v):
        if b == 1:
            return v
        return jnp.broadcast_to(v[:, None, :], (b, t, d)).reshape(m, d)

    for li in range(depth):
        shift, scale, gate = mods[li]
        h = _norm(x, p["norm_w"][li], rows(scale), rows(shift), BF16, tm)
        if li % 2 == 0:
            e = li // 2
            w_in = p["w_in_even"][e]
            lam_init = 0.8 - 0.6 * math.exp(-0.3 * li)
            lam = (jnp.exp(jnp.sum(p["lambda_q1"][e] * p["lambda_k1"][e]))
                   - jnp.exp(jnp.sum(p["lambda_q2"][e] * p["lambda_k2"][e])) + lam_init)
            scal = jnp.stack([lam, jnp.float32(1.0 - lam_init)]).reshape(1, 2).astype(F32)
            sw = p["subln_w"][e].reshape(1, -1)
            fz = _proj_f32(h, w_in, (3, 7, 8), tn, tm)
            qkv_b = _proj_bf16(h, w_in, (4, 5, 6), (1.0, float(LANES) ** -0.5, 1.0), tn, tm)
            ng = 2 * hb_heads
            w_g = jnp.pad(w_in[:, 9 * tn:9 * tn + ng], ((0, 0), (0, LANES - ng)))
            b_g = jnp.pad(p["b_gates_even"][e], (0, LANES - ng)).reshape(1, LANES)
            gates = _proj_bias(h, w_g, b_g, tm)
            if prompt:
                q1, q2 = _proj_q(h, w_in, 0, tn, tm, da, True)
                k_f, k_b = _proj_k(h, w_in, 1, tn, tm)
                v_f, v_t = _proj_vt(h, w_in, 2, tn, tm, heads, tk)
                oa = _attn_prompt(scal, q1, q2, k_b, v_t, fz, sw, heads, tq)
                dh = wb // hb_heads
                c0 = jnp.zeros((b, hb_heads, dh, dh), F32)
                n0 = jnp.zeros((b, hb_heads, dh), F32)
                m0 = jnp.zeros((b, 1, LANES), F32)
                L, cpb = CHUNK, 2
            else:
                q1, q2 = _proj_q(h, w_in, 0, tn, tm, da, False)
                kv = _proj_f32(h, w_in, (1, 2), tn, tm)
                k_f, v_f = kv[0], kv[1]
                ck = cache["k"][e].reshape(b, -1, heads * 2 * da)
                cv = cache["v"][e].reshape(b, -1, heads * 2 * da)
                oa = _attn_sample(scal, q1, q2, ck, cv, kv, fz, sw, heads, t)
                c0 = cache["c"][e]
                n0 = cache["n"][e]
                m0 = jnp.pad(cache["m"][e], ((0, 0), (0, LANES - hb_heads))).reshape(b, 1, LANES)
                L, cpb = t, 1
            hb, c_n, n_n, m_n = _mlstm(qkv_b, fz, gates, c0, n0, m0,
                                       p["mh_norm_w"][e].reshape(1, -1), b, L, cpb)
            k_list.append(k_f.reshape(b, t, heads, 2, da))
            v_list.append(v_f.reshape(b, t, heads, 2 * da))
            mc_list.append(c_n)
            mn_list.append(n_n)
            mm_list.append(m_n[:, 0, :hb_heads])
            x = _outproj([oa, hb], p["w_out_even"][e], x, rows(gate), tm)
        else:
            o = li // 2
            w_in = p["w_in_odd"][o]
            dr = w_in.shape[1] // 2
            xz = _proj_f32(h, w_in, (0, 1), dr, _tile(m, 512))
            wg = jnp.concatenate([p["rg_wa"][o], p["rg_wx"][o]], axis=-1).astype(BF16)
            if cache is None:
                buf0 = jnp.zeros((b, SUBLANES, dr), F32)
                h0 = jnp.zeros((b, 1, dr), F32)
            else:
                buf0 = jnp.pad(cache["conv"][o], ((0, 0), (SUBLANES - (CONV_W - 1), 0), (0, 0)))
                h0 = cache["h"][o].reshape(b, 1, dr)
            y, cs, hl = _rglru(xz, p["conv_w"][o], p["conv_b"][o].reshape(1, dr), wg,
                               p["rg_ba"][o].reshape(1, dr), p["rg_bx"][o].reshape(1, dr),
                               p["rg_lambda"][o].reshape(1, dr), buf0, h0, b, _tile(t, 256))
            conv_list.append(cs[:, SUBLANES - (CONV_W - 1):, :])
            hl_list.append(hl[:, 0, :])
            x = _outproj([y], p["w_out_odd"][o], x, rows(gate), tm)
    zero = jnp.zeros((1, d), F32)
    y = _norm(x, p["final_w"], zero, zero, F32, tm).reshape(b, t, d)
    ev = [jnp.stack(s) for s in (k_list, v_list, mc_list, mn_list, mm_list)]
    od = [jnp.stack(s) for s in (conv_list, hl_list)]
    return y, ev, od


def kernel(x_prompt, x_sample, c_prompt, c_sample, cache_k, cache_v, state_mlstm_c, state_mlstm_n, state_mlstm_m, state_conv, state_rglru, norm_w, w_ada, b_ada, w_in_even, b_gates_even, lambda_q1, lambda_k1, lambda_q2, lambda_k2, subln_w, mh_norm_w, w_out_even, w_in_odd, conv_w, conv_b, rg_wa, rg_ba, rg_wx, rg_bx, rg_lambda, w_out_odd, final_w):
    p = dict(norm_w=norm_w, w_in_even=w_in_even, b_gates_even=b_gates_even,
             lambda_q1=lambda_q1, lambda_k1=lambda_k1, lambda_q2=lambda_q2, lambda_k2=lambda_k2,
             subln_w=subln_w, mh_norm_w=mh_norm_w, w_out_even=w_out_even, w_in_odd=w_in_odd,
             conv_w=conv_w, conv_b=conv_b, rg_wa=rg_wa, rg_ba=rg_ba, rg_wx=rg_wx, rg_bx=rg_bx,
             rg_lambda=rg_lambda, w_out_odd=w_out_odd, final_w=final_w)
    depth, d = norm_w.shape
    bp, bs = c_prompt.shape[0], c_sample.shape[0]
    rpad = -(bp + bs) % SUBLANES
    c_all = jnp.pad(jnp.concatenate([c_prompt, c_sample], axis=0), ((0, rpad), (0, 0)))
    mod = _modulation(c_all, w_ada, b_ada)

    def mods_for(lo, hi):
        return [tuple(mod[li, lo:hi, i * d:(i + 1) * d] for i in range(3)) for li in range(depth)]

    y_p, ev_p, od_p = _trunk(x_prompt, mods_for(0, bp), p, None, True)
    cache = dict(k=cache_k, v=cache_v, c=state_mlstm_c, n=state_mlstm_n, m=state_mlstm_m,
                 conv=state_conv, h=state_rglru)
    y_s, ev_s, od_s = _trunk(x_sample, mods_for(bp, bp + bs), p, cache, False)
    return (y_p, y_s, *ev_p, *od_p, *ev_s, *od_s)
```

```python
import functools
import math

import jax
import jax.numpy as jnp
from jax import lax
from jax.experimental import pallas as pl
from jax.experimental.pallas import tpu as pltpu

F32 = jnp.float32
BF16 = jnp.bfloat16
EPS = 1e-6
CHUNK = 64
C_RG = 8.0
CONV_W = 4
LANES = 128
SUBLANES = 8
ONES_ROWS = 16
NT_DIMS = (((1,), (1,)), ((), ()))
TN_DIMS = (((0,), (0,)), ((), ()))
HIGHEST = lax.Precision.HIGHEST
VMEM_LIMIT = 56 * 1024 * 1024


def _params(sem):
    return pltpu.CompilerParams(dimension_semantics=sem, vmem_limit_bytes=VMEM_LIMIT)


def _silu(z):
    return z * jax.nn.sigmoid(z)


def _sigmoid_t(z):
    return 0.5 * jnp.tanh(0.5 * z) + 0.5


def _mod_kernel(c_ref, w_ref, b_ref, o_ref):
    sc = _silu(c_ref[...]).astype(BF16)
    o_ref[0] = jnp.dot(sc, w_ref[0].astype(BF16), preferred_element_type=F32) + b_ref[0]


def _modulation(c_all, w_ada, b_ada):
    depth, d, n = w_ada.shape
    r = c_all.shape[0]
    tn = 1024
    return pl.pallas_call(
        _mod_kernel,
        grid=(depth, n // tn),
        in_specs=[
            pl.BlockSpec((r, d), lambda l, j: (0, 0)),
            pl.BlockSpec((1, d, tn), lambda l, j: (l, 0, j)),
            pl.BlockSpec((1, 1, tn), lambda l, j: (l, 0, j)),
        ],
        out_specs=pl.BlockSpec((1, r, tn), lambda l, j: (l, 0, j)),
        out_shape=jax.ShapeDtypeStruct((depth, r, n), F32),
        compiler_params=_params(("arbitrary", "arbitrary")),
        name="adaln_mod",
    )(c_all, w_ada, b_ada.reshape(depth, 1, n))


def _norm_kernel(x_ref, nw_ref, sc_ref, sh_ref, o_ref):
    x = x_ref[...]
    ms = jnp.mean(x * x, axis=-1, keepdims=True)
    y = (x * lax.rsqrt(ms + EPS)) * nw_ref[...]
    o_ref[...] = (y * (1.0 + sc_ref[...]) + sh_ref[...]).astype(o_ref.dtype)


def _row_spec(arr, tm):
    if arr.shape[0] == 1:
        return pl.BlockSpec((1, arr.shape[1]), lambda i: (0, 0))
    return pl.BlockSpec((tm, arr.shape[1]), lambda i: (i, 0))


def _norm(x, nw, scale, shift, out_dtype, tm):
    m, d = x.shape
    return pl.pallas_call(
        _norm_kernel,
        grid=(m // tm,),
        in_specs=[
            pl.BlockSpec((tm, d), lambda i: (i, 0)),
            pl.BlockSpec((1, d), lambda i: (0, 0)),
            _row_spec(scale, tm),
            _row_spec(shift, tm),
        ],
        out_specs=pl.BlockSpec((tm, d), lambda i: (i, 0)),
        out_shape=jax.ShapeDtypeStruct((m, d), out_dtype),
        compiler_params=_params(("arbitrary",)),
        name="adaln_norm",
    )(x, nw.reshape(1, d), scale, shift)


def _proj_acc(h_ref, w_ref, wbf_ref):
    @pl.when(pl.program_id(1) == 0)
    def _():
        wbf_ref[...] = w_ref[...].astype(BF16)

    return jnp.dot(h_ref[...], wbf_ref[...], preferred_element_type=F32)


def _proj_f32_kernel(h_ref, w_ref, o_ref, wbf_ref):
    o_ref[...] = _proj_acc(h_ref, w_ref, wbf_ref)


def _proj_bias_kernel(h_ref, w_ref, b_ref, o_ref, wbf_ref):
    o_ref[...] = _proj_acc(h_ref, w_ref, wbf_ref) + b_ref[...]


def _proj_bf16_kernel(h_ref, w_ref, o_ref, wbf_ref, *, scales):
    acc = _proj_acc(h_ref, w_ref, wbf_ref)
    j = pl.program_id(0)
    s = jnp.float32(scales[0])
    for t in range(1, len(scales)):
        s = jnp.where(j == t, jnp.float32(scales[t]), s)
    o_ref[...] = (acc * s).astype(BF16)


def _proj_k_kernel(h_ref, w_ref, o_ref, obf_ref, wbf_ref):
    acc = _proj_acc(h_ref, w_ref, wbf_ref)
    o_ref[...] = acc
    obf_ref[...] = acc.astype(BF16)


def _proj_vt_kernel(h_ref, w_ref, o_ref, ot_ref, wbf_ref, *, tk):
    acc = _proj_acc(h_ref, w_ref, wbf_ref)
    o_ref[...] = acc
    at = acc.T.astype(BF16)
    nh, nt = ot_ref.shape[0], ot_ref.shape[1]
    va = ot_ref.shape[2] - ONES_ROWS
    ones = (lax.broadcasted_iota(jnp.int32, (ONES_ROWS, tk), 0) == 0).astype(BF16)
    for hh in range(nh):
        for t in range(nt):
            ot_ref[hh, t, 0:va, :] = at[hh * va:(hh + 1) * va, t * tk:(t + 1) * tk]
            ot_ref[hh, t, va:va + ONES_ROWS, :] = ones


def _q_split(acc, axis, da, qk_scale):
    idx = lax.broadcasted_iota(jnp.int32, acc.shape, axis)
    first = (idx % (2 * da)) < da
    a = acc * qk_scale
    zero = jnp.zeros_like(a)
    return jnp.where(first, a, zero).astype(BF16), jnp.where(first, zero, a).astype(BF16)


def _proj_q_kernel(h_ref, w_ref, o1_ref, o2_ref, wbf_ref, *, da, qk_scale):
    acc = _proj_acc(h_ref, w_ref, wbf_ref)
    o1_ref[...], o2_ref[...] = _q_split(acc, 1, da, qk_scale)


def _proj_qt_kernel(h_ref, w_ref, o1_ref, o2_ref, wbf_ref, *, da, qk_scale):
    acc = _proj_acc(h_ref, w_ref, wbf_ref)
    o1_ref[...], o2_ref[...] = _q_split(acc.T, 0, da, qk_scale)


def _col_index(cols, j):
    idx = jnp.int32(cols[0])
    for t in range(1, len(cols)):
        idx = jnp.where(j == t, jnp.int32(cols[t]), idx)
    return idx


def _proj_call(body, h, w, cols, tn, tm, out_shapes, out_specs, extra_in=(), extra_specs=(), name="proj"):
    m, d = h.shape
    nj = len(cols)
    return pl.pallas_call(
        body,
        grid=(nj, m // tm),
        in_specs=[
            pl.BlockSpec((tm, d), lambda j, i: (i, 0)),
            pl.BlockSpec((d, tn), lambda j, i: (0, _col_index(cols, j))),
            *extra_specs,
        ],
        out_specs=out_specs,
        out_shape=out_shapes,
        scratch_shapes=[pltpu.VMEM((d, tn), BF16)],
        compiler_params=_params(("arbitrary", "arbitrary")),
        name=name,
    )(h, w, *extra_in)


def _proj_f32(h, w, cols, tn, tm):
    m = h.shape[0]
    return _proj_call(
        _proj_f32_kernel, h, w, cols, tn, tm,
        jax.ShapeDtypeStruct((len(cols), m, tn), F32),
        pl.BlockSpec((None, tm, tn), lambda j, i: (j, i, 0)), name="proj_f32")


def _proj_bf16(h, w, cols, scales, tn, tm):
    m = h.shape[0]
    return _proj_call(
        functools.partial(_proj_bf16_kernel, scales=tuple(scales)), h, w, cols, tn, tm,
        jax.ShapeDtypeStruct((len(cols), m, tn), BF16),
        pl.BlockSpec((None, tm, tn), lambda j, i: (j, i, 0)), name="proj_bf16")


def _proj_bias(h, w_small, bias, tm):
    m = h.shape[0]
    tn = w_small.shape[1]
    return _proj_call(
        _proj_bias_kernel, h, w_small, (0,), tn, tm,
        jax.ShapeDtypeStruct((m, tn), F32),
        pl.BlockSpec((tm, tn), lambda j, i: (i, 0)),
        extra_in=(bias,), extra_specs=(pl.BlockSpec((1, tn), lambda j, i: (0, 0)),), name="proj_gates")


def _proj_k(h, w, col, tn, tm):
    m = h.shape[0]
    return _proj_call(
        _proj_k_kernel, h, w, (col,), tn, tm,
        (jax.ShapeDtypeStruct((m, tn), F32), jax.ShapeDtypeStruct((m, tn), BF16)),
        (pl.BlockSpec((tm, tn), lambda j, i: (i, 0)), pl.BlockSpec((tm, tn), lambda j, i: (i, 0))),
        name="proj_k")


def _proj_vt(h, w, col, tn, tm, heads, tk):
    m = h.shape[0]
    va = tn // heads
    return _proj_call(
        functools.partial(_proj_vt_kernel, tk=tk), h, w, (col,), tn, tm,
        (jax.ShapeDtypeStruct((m, tn), F32),
         jax.ShapeDtypeStruct((heads, m // tk, va + ONES_ROWS, tk), BF16)),
        (pl.BlockSpec((tm, tn), lambda j, i: (i, 0)),
         pl.BlockSpec((heads, tm // tk, va + ONES_ROWS, tk), lambda j, i: (0, i, 0, 0))),
        name="proj_vt")


def _proj_q(h, w, col, tn, tm, da, transposed):
    m = h.shape[0]
    qk_scale = float(da) ** -0.5
    if transposed:
        qk_scale *= math.log2(math.e)
        body = functools.partial(_proj_qt_kernel, da=da, qk_scale=qk_scale)
        shape = jax.ShapeDtypeStruct((tn, m), BF16)
        spec = pl.BlockSpec((tn, tm), lambda j, i: (0, i))
    else:
        body = functools.partial(_proj_q_kernel, da=da, qk_scale=qk_scale)
        shape = jax.ShapeDtypeStruct((m, tn), BF16)
        spec = pl.BlockSpec((tm, tn), lambda j, i: (i, 0))
    return _proj_call(body, h, w, (col,), tn, tm, (shape, shape), (spec, spec), name="proj_q")


def _attn_finish(o, z, sw, post):
    ms = jnp.mean(o * o, axis=-1, keepdims=True)
    on = (o * lax.rsqrt(ms + EPS)) * sw
    return (on * post) * _silu(z)


def _attn_prompt_kernel(sc_ref, q1_ref, q2_ref, k_ref, vt_ref, za_ref, sw_ref, o_ref,
                        m_ref, acc_ref, sa_ref, sb_ref, cma_ref, cmb_ref, *, tq, tk):
    i = pl.program_id(1)
    q_refs = (q1_ref, q2_ref)
    m_ref[...] = jnp.full(m_ref.shape, -jnp.inf, F32)
    acc_ref[...] = jnp.zeros(acc_ref.shape, F32)
    r = tq // tk
    va = o_ref.shape[1]

    def scores(j, c, lo):
        kt = k_ref[pl.ds(pl.multiple_of(j * tk, tk), tk), :]
        return jnp.dot(kt, q_refs[c][:, lo:], preferred_element_type=F32)

    def update(j, c, s, smax, lo):
        vt = vt_ref[j]
        m_old = m_ref[c, :, lo:]
        m_new = jnp.maximum(m_old, smax)
        alpha = jnp.exp2(m_old - m_new)
        p = jnp.exp2(s - m_new)
        acc_ref[c, :, lo:] = (alpha * acc_ref[c, :, lo:]
                              + jnp.dot(vt, p.astype(BF16), preferred_element_type=F32))
        m_ref[c, :, lo:] = m_new

    def fill(j, buf, lo=0):
        s_ref, cm_ref = buf
        for c in range(2):
            s = scores(j, c, lo)
            s_ref[c, :, lo:] = s
            cm_ref[c, :, lo:] = jnp.max(s, axis=0, keepdims=True)

    def drain(j, buf, lo=0, masked=False):
        s_ref, cm_ref = buf
        for c in range(2):
            s = s_ref[c, :, lo:]
            if masked:
                kc = lax.broadcasted_iota(jnp.int32, s.shape, 0) // CHUNK
                qc = lax.broadcasted_iota(jnp.int32, s.shape, 1) // CHUNK
                s = jnp.where(kc <= qc, s, -jnp.inf)
                smax = jnp.max(s, axis=0, keepdims=True)
            else:
                smax = cm_ref[c, :, lo:]
            update(j, c, s, smax, lo)

    n = i * r
    bufs = ((sa_ref, cma_ref), (sb_ref, cmb_ref))
    fill(0, bufs[0])

    def body(t, carry):
        j = 2 * t
        fill(j + 1, bufs[1])
        drain(j, bufs[0])
        fill(j + 2, bufs[0])
        drain(j + 1, bufs[1])
        return carry

    lax.fori_loop(0, n // 2, body, 0)
    for dd in range(r):
        if dd + 1 < r:
            fill(n + dd + 1, bufs[(dd + 1) % 2], (dd + 1) * tk)
        drain(n + dd, bufs[dd % 2], dd * tk, masked=True)

    lam = sc_ref[0, 0]
    post = sc_ref[0, 1]
    o = (acc_ref[0, 0:va, :] / acc_ref[0, va:va + 1, :]
         - lam * (acc_ref[1, 0:va, :] / acc_ref[1, va:va + 1, :]))
    o_ref[...] = _attn_finish(o.T, za_ref[...], sw_ref[...], post).astype(o_ref.dtype)


def _attn_prompt(scal, q1t, q2t, kbf, vt, fz, sw, heads, tq):
    dq, m = q1t.shape
    hd = dq // heads
    tk = vt.shape[3]
    hv = vt.shape[2]
    assert tq % (2 * tk) == 0 and m % tq == 0
    nk = m // tk
    return pl.pallas_call(
        functools.partial(_attn_prompt_kernel, tq=tq, tk=tk),
        grid=(heads, m // tq),
        in_specs=[
            pl.BlockSpec(memory_space=pltpu.SMEM),
            pl.BlockSpec((hd, tq), lambda h, i: (h, i)),
            pl.BlockSpec((hd, tq), lambda h, i: (h, i)),
            pl.BlockSpec((m, hd), lambda h, i: (0, h)),
            pl.BlockSpec((None, nk, hv, tk), lambda h, i: (h, 0, 0, 0)),
            pl.BlockSpec((None, tq, hd), lambda h, i: (0, i, h)),
            pl.BlockSpec((1, hd), lambda h, i: (0, 0)),
        ],
        out_specs=pl.BlockSpec((tq, hd), lambda h, i: (i, h)),
        out_shape=jax.ShapeDtypeStruct((m, dq), BF16),
        scratch_shapes=[
            pltpu.VMEM((2, 1, tq), F32),
            pltpu.VMEM((2, hv, tq), F32),
            pltpu.VMEM((2, tk, tq), F32),
            pltpu.VMEM((2, tk, tq), F32),
            pltpu.VMEM((2, 1, tq), F32),
            pltpu.VMEM((2, 1, tq), F32),
        ],
        compiler_params=_params(("arbitrary", "arbitrary")),
        name="diff_attn_prompt",
    )(scal, q1t, q2t, kbf, vt, fz, sw)


def _attn_sample_kernel(sc_ref, q1_ref, q2_ref, ck_ref, cv_ref, kn_ref, vn_ref, za_ref, sw_ref, o_ref,
                        *, heads):
    hd = o_ref.shape[1] // heads
    lam = sc_ref[0, 0]
    post = sc_ref[0, 1]
    for h in range(heads):
        hs = slice(h * hd, (h + 1) * hd)
        ck = ck_ref[:, hs].astype(BF16)
        cv = cv_ref[:, hs].astype(BF16)
        kn = kn_ref[:, hs].astype(BF16)
        vn = vn_ref[:, hs].astype(BF16)
        outs = []
        for q_ref in (q1_ref, q2_ref):
            q = q_ref[:, hs]
            sc = lax.dot_general(q, ck, NT_DIMS, preferred_element_type=F32)
            sn = lax.dot_general(q, kn, NT_DIMS, preferred_element_type=F32)
            mx = jnp.maximum(jnp.max(sc, axis=-1, keepdims=True), jnp.max(sn, axis=-1, keepdims=True))
            pc = jnp.exp(sc - mx)
            pn = jnp.exp(sn - mx)
            l = jnp.sum(pc, axis=-1, keepdims=True) + jnp.sum(pn, axis=-1, keepdims=True)
            acc = (jnp.dot(pc.astype(BF16), cv, preferred_element_type=F32)
                   + jnp.dot(pn.astype(BF16), vn, preferred_element_type=F32))
            outs.append(acc / l)
        o = outs[0] - lam * outs[1]
        o_ref[:, hs] = _attn_finish(o, za_ref[:, hs], sw_ref[...], post).astype(o_ref.dtype)


def _attn_sample(scal, q1, q2, ck, cv, kv, fz, sw, heads, t):
    m, dq = q1.shape
    hd = dq // heads
    b, p, _ = ck.shape
    return pl.pallas_call(
        functools.partial(_attn_sample_kernel, heads=heads),
        grid=(b,),
        in_specs=[
            pl.BlockSpec(memory_space=pltpu.SMEM),
            pl.BlockSpec((t, dq), lambda bb: (bb, 0)),
            pl.BlockSpec((t, dq), lambda bb: (bb, 0)),
            pl.BlockSpec((None, p, dq), lambda bb: (bb, 0, 0)),
            pl.BlockSpec((None, p, dq), lambda bb: (bb, 0, 0)),
            pl.BlockSpec((None, t, dq), lambda bb: (0, bb, 0)),
            pl.BlockSpec((None, t, dq), lambda bb: (1, bb, 0)),
            pl.BlockSpec((None, t, dq), lambda bb: (0, bb, 0)),
            pl.BlockSpec((1, hd), lambda bb: (0, 0)),
        ],
        out_specs=pl.BlockSpec((t, dq), lambda bb: (bb, 0)),
        out_shape=jax.ShapeDtypeStruct((m, dq), BF16),
        compiler_params=_params(("arbitrary",)),
        name="diff_attn_sample",
    )(scal, q1, q2, ck, cv, kv, kv, fz, sw)


def _mlstm_kernel(q_ref, k_ref, v_ref, ob_ref, zb_ref, g_ref, c0_ref, n0_ref, m0_ref, w_ref,
                  hb_ref, c_out, n_out, m_out, ct_scr, m_scr, *, L, cpb, H, DH):
    t = pl.program_id(1)

    @pl.when(t == 0)
    def _():
        for h in range(H):
            ct_scr[h, 0:DH, :] = c0_ref[0, h].T
            ct_scr[h, DH:DH + SUBLANES, :] = jnp.zeros((SUBLANES, DH), F32)
            ct_scr[h, DH:DH + 1, :] = n0_ref[0, h:h + 1, :]
        m_scr[...] = m0_ref[0]

    src = lax.broadcasted_iota(jnp.int32, (L, L), 0)
    tgt = lax.broadcasted_iota(jnp.int32, (L, L), 1)
    visible = src <= tgt
    tril = (tgt <= src).astype(F32)
    sel = (lax.broadcasted_iota(jnp.int32, (2 * H, LANES), 0)
           == lax.broadcasted_iota(jnp.int32, (2 * H, LANES), 1)).astype(F32)
    lane = lax.broadcasted_iota(jnp.int32, (1, LANES), 1)

    for cc in range(cpb):
        r0 = cc * L
        g = g_ref[r0:r0 + L, :]
        lf = jax.nn.log_sigmoid(g)
        bcum = jnp.dot(tril, lf, precision=HIGHEST, preferred_element_type=F32)
        x = jnp.where(lane < H, g, bcum)
        xt = lax.dot_general(sel, x, NT_DIMS, precision=HIGHEST, preferred_element_type=F32)
        csx = g - pltpu.roll(bcum, LANES - H, 1)
        m_all = m_scr[...]
        m_next = m_all
        for h in range(H):
            b_row = xt[H + h:H + h + 1, :]
            cs = jnp.broadcast_to(csx[:, h:h + 1], (L, DH))
            m0 = m_all[:, h:h + 1]
            dmat = jnp.where(visible, b_row + cs[:, 0:L], -jnp.inf)
            m_row = jnp.maximum(b_row + m0, jnp.max(dmat, axis=0, keepdims=True))
            wgt = jnp.exp(dmat - m_row)
            inter = jnp.exp(b_row + m0 - m_row)
            hs = slice(h * DH, (h + 1) * DH)
            qh = q_ref[r0:r0 + L, hs]
            kh = k_ref[r0:r0 + L, hs]
            vh = v_ref[r0:r0 + L, hs]
            st = lax.dot_general(kh, qh, NT_DIMS, preferred_element_type=F32) * wgt
            ctn = ct_scr[h]
            carried = lax.dot_general(ctn.astype(BF16), qh, NT_DIMS,
                                      preferred_element_type=F32)
            num = (lax.dot_general(vh, st.astype(BF16), TN_DIMS, preferred_element_type=F32)
                   + inter * carried[0:DH, :])
            den = jnp.sum(st, axis=0, keepdims=True) + inter * carried[DH:DH + 1, :]
            denom = jnp.maximum(jnp.abs(den), jnp.exp(-m_row))
            hh = (num / denom).T
            hh = jax.nn.sigmoid(ob_ref[r0:r0 + L, hs]) * hh
            ms = jnp.mean(hh * hh, axis=-1, keepdims=True)
            hn = (hh * lax.rsqrt(ms + EPS)) * w_ref[:, hs]
            hb_ref[r0:r0 + L, hs] = (hn * _silu(zb_ref[r0:r0 + L, hs])).astype(hb_ref.dtype)
            b_last = b_row[:, L - 1:L]
            m_last = m_row[:, L - 1:L]
            ws = jnp.exp(cs + (b_last - m_last))
            decay = jnp.exp(b_last + m0 - m_last)
            wv = (ws * vh.astype(F32)).astype(BF16)
            ct_scr[h, 0:DH, :] = (decay * ctn[0:DH, :]
                                  + lax.dot_general(wv, kh, TN_DIMS, preferred_element_type=F32))
            ct_scr[h, DH:DH + 1, :] = (decay * ctn[DH:DH + 1, :]
                                       + jnp.sum(ws * kh.astype(F32), axis=0, keepdims=True))
            m_next = jnp.where(lane == h, m_last, m_next)
        m_scr[...] = m_next

    @pl.when(t == pl.num_programs(1) - 1)
    def _():
        for h in range(H):
            c_out[0, h] = ct_scr[h, 0:DH, :].T
            n_out[0, h:h + 1, :] = ct_scr[h, DH:DH + 1, :]
        m_out[0] = m_scr[...]


def _mlstm(qkv, fz, gates, c0, n0, m0, mh_w, batch, L, cpb):
    _, m, wb = qkv.shape
    _, H, DH, _ = c0.shape
    rows = L * cpb
    nt = m // batch // rows
    blk = lambda which: pl.BlockSpec((None, rows, wb), lambda b, t: (which, b * nt + t, 0))
    return pl.pallas_call(
        functools.partial(_mlstm_kernel, L=L, cpb=cpb, H=H, DH=DH),
        grid=(batch, nt),
        in_specs=[
            blk(0), blk(1), blk(2),
            blk(1), blk(2),
            pl.BlockSpec((rows, LANES), lambda b, t: (b * nt + t, 0)),
            pl.BlockSpec((1, H, DH, DH), lambda b, t: (b, 0, 0, 0)),
            pl.BlockSpec((1, H, DH), lambda b, t: (b, 0, 0)),
            pl.BlockSpec((1, 1, LANES), lambda b, t: (b, 0, 0)),
            pl.BlockSpec((1, wb), lambda b, t: (0, 0)),
        ],
        out_specs=(
            pl.BlockSpec((rows, wb), lambda b, t: (b * nt + t, 0)),
            pl.BlockSpec((1, H, DH, DH), lambda b, t: (b, 0, 0, 0)),
            pl.BlockSpec((1, H, DH), lambda b, t: (b, 0, 0)),
            pl.BlockSpec((1, 1, LANES), lambda b, t: (b, 0, 0)),
        ),
        out_shape=(
            jax.ShapeDtypeStruct((m, wb), BF16),
            jax.ShapeDtypeStruct((batch, H, DH, DH), F32),
            jax.ShapeDtypeStruct((batch, H, DH), F32),
            jax.ShapeDtypeStruct((batch, 1, LANES), F32),
        ),
        scratch_shapes=[
            pltpu.VMEM((H, DH + SUBLANES, DH), F32),
            pltpu.VMEM((1, LANES), F32),
        ],
        compiler_params=_params(("arbitrary", "arbitrary")),
        name="mlstm_chunks",
    )(qkv, qkv, qkv, fz, fz, gates, c0, n0, m0, mh_w)


def _outproj_kernel(*refs, n_in):
    a_refs = refs[:n_in]
    w_ref, x_ref, g_ref, o_ref, wbf_ref = refs[n_in:]

    @pl.when(pl.program_id(0) == 0)
    def _():
        wbf_ref[...] = w_ref[...].astype(BF16)

    acc = None
    off = 0
    for a_ref in a_refs:
        kk = a_ref.shape[1]
        part = jnp.dot(a_ref[...], wbf_ref[off:off + kk, :], preferred_element_type=F32)
        acc = part if acc is None else acc + part
        off += kk
    o_ref[...] = x_ref[...] + g_ref[...] * acc


def _outproj(acts, w, x, gate, tm):
    m, d = x.shape
    kt = w.shape[0]
    return pl.pallas_call(
        functools.partial(_outproj_kernel, n_in=len(acts)),
        grid=(m // tm,),
        in_specs=[
            *[pl.BlockSpec((tm, a.shape[1]), lambda i: (i, 0)) for a in acts],
            pl.BlockSpec((kt, d), lambda i: (0, 0)),
            pl.BlockSpec((tm, d), lambda i: (i, 0)),
            _row_spec(gate, tm),
        ],
        out_specs=pl.BlockSpec((tm, d), lambda i: (i, 0)),
        out_shape=jax.ShapeDtypeStruct((m, d), F32),
        scratch_shapes=[pltpu.VMEM((kt, d), BF16)],
        compiler_params=_params(("arbitrary",)),
        name="outproj_residual",
    )(*acts, w, x, gate)


def _rglru_kernel(xr_ref, zr_ref, cw_ref, cb_ref, wg_ref, ba_ref, bx_ref, lam_ref, buf0_ref, h0_ref,
                  y_ref, cs_ref, hl_ref, xbuf, a_scr, u_scr, hcar, *, tt, nblk, bw):
    t = pl.program_id(1)
    pad = SUBLANES

    @pl.when(t == 0)
    def _():
        xbuf[0:pad, :] = buf0_ref[0]
        hcar[...] = h0_ref[0]

    xbuf[pad:pad + tt, :] = xr_ref[...]
    xc = cb_ref[...] + cw_ref[CONV_W - 1:CONV_W, :] * xbuf[pad:pad + tt, :]
    for j in range(1, CONV_W):
        xc = xc + cw_ref[CONV_W - 1 - j:CONV_W - j, :] * xbuf[pad - j:pad - j + tt, :]

    neg_c = -C_RG * jax.nn.softplus(-lam_ref[...])
    for hb in range(nblk):
        cs = slice(hb * bw, (hb + 1) * bw)
        xb = xc[:, cs]
        gg = jnp.dot(xb.astype(BF16), wg_ref[hb], preferred_element_type=F32)
        r = _sigmoid_t(gg[:, :bw] + ba_ref[:, cs])
        ig = _sigmoid_t(gg[:, bw:] + bx_ref[:, cs])
        log_a = neg_c[:, cs] * r
        th = jnp.tanh(log_a)
        one_minus_a2 = (-2.0 * th) / (1.0 - th)
        a_scr[:, cs] = jnp.exp(log_a)
        u_scr[:, cs] = jnp.sqrt(one_minus_a2) * (ig * xb)

    rowi = lax.broadcasted_iota(jnp.int32, (SUBLANES, a_scr.shape[1]), 0)

    def group(gi, hc):
        r0 = pl.multiple_of(gi * SUBLANES, SUBLANES)
        a = a_scr[pl.ds(r0, SUBLANES), :]
        u = u_scr[pl.ds(r0, SUBLANES), :]
        for dd in (1, 2, 4):
            ok = rowi >= dd
            u = jnp.where(ok, a * pltpu.roll(u, dd, 0) + u, u)
            a = jnp.where(ok, a * pltpu.roll(a, dd, 0), a)
        hgrp = u + a * hc
        u_scr[pl.ds(r0, SUBLANES), :] = hgrp
        return hgrp[SUBLANES - 1:SUBLANES, :]

    hc = lax.fori_loop(0, tt // SUBLANES, group, hcar[...])
    hcar[...] = hc
    zr = zr_ref[...]
    y_ref[...] = (u_scr[...] * (zr * _sigmoid_t(zr))).astype(y_ref.dtype)
    xbuf[0:pad, :] = xbuf[tt:tt + pad, :]

    @pl.when(t == pl.num_programs(1) - 1)
    def _():
        cs_ref[0] = xbuf[0:pad, :]
        hl_ref[0] = hc


def _rglru(xz, cw, cb, wg, ba, bx, lam, buf0, h0, batch, tt):
    _, m, dr = xz.shape
    nblk, bw, _ = wg.shape
    nt = m // batch // tt
    vec = lambda: pl.BlockSpec((1, dr), lambda b, t: (0, 0))
    return pl.pallas_call(
        functools.partial(_rglru_kernel, tt=tt, nblk=nblk, bw=bw),
        grid=(batch, nt),
        in_specs=[
            pl.BlockSpec((None, tt, dr), lambda b, t: (0, b * nt + t, 0)),
            pl.BlockSpec((None, tt, dr), lambda b, t: (1, b * nt + t, 0)),
            pl.BlockSpec((CONV_W, dr), lambda b, t: (0, 0)),
            vec(),
            pl.BlockSpec((nblk, bw, 2 * bw), lambda b, t: (0, 0, 0)),
            vec(), vec(), vec(),
            pl.BlockSpec((1, SUBLANES, dr), lambda b, t: (b, 0, 0)),
            pl.BlockSpec((1, 1, dr), lambda b, t: (b, 0, 0)),
        ],
        out_specs=(
            pl.BlockSpec((tt, dr), lambda b, t: (b * nt + t, 0)),
            pl.BlockSpec((1, SUBLANES, dr), lambda b, t: (b, 0, 0)),
            pl.BlockSpec((1, 1, dr), lambda b, t: (b, 0, 0)),
        ),
        out_shape=(
            jax.ShapeDtypeStruct((m, dr), BF16),
            jax.ShapeDtypeStruct((batch, SUBLANES, dr), F32),
            jax.ShapeDtypeStruct((batch, 1, dr), F32),
        ),
        scratch_shapes=[
            pltpu.VMEM((tt + SUBLANES, dr), F32),
            pltpu.VMEM((tt, dr), F32),
            pltpu.VMEM((tt, dr), F32),
            pltpu.VMEM((1, dr), F32),
        ],
        compiler_params=_params(("arbitrary", "arbitrary")),
        name="rglru_block",
    )(xz, xz, cw, cb, wg, ba, bx, lam, buf0, h0)


def _tile(m, pref):
    t = min(m, pref)
    assert m % t == 0
    return t


def _trunk(x3, mods, p, cache, prompt):
    b, t, d = x3.shape
    m = b * t
    depth = p["norm_w"].shape[0]
    x = x3.reshape(m, d)
    tm = _tile(m, 1024)
    va = p["subln_w"].shape[1]
    da = va // 2
    wb = p["mh_norm_w"].shape[1]
    tn = p["w_out_even"].shape[1] - wb
    assert tn == wb
    heads = tn // va
    hb_heads = p["b_gates_even"].shape[1] // 2
    tk = 512
    tq = _tile(m, 1024)
    k_list, v_list, mc_list, mn_list, mm_list, conv_list, hl_list = [], [], [], [], [], [], []

    def rows(v):
        if b == 1:
            return v
        return jnp.broadcast_to(v[:, None, :], (b, t, d)).reshape(m, d)

    for li in range(depth):
        shift, scale, gate = mods[li]
        h = _norm(x, p["norm_w"][li], rows(scale), rows(shift), BF16, tm)
        if li % 2 == 0:
            e = li // 2
            w_in = p["w_in_even"][e]
            lam_init = 0.8 - 0.6 * math.exp(-0.3 * li)
            lam = (jnp.exp(jnp.sum(p["lambda_q1"][e] * p["lambda_k1"][e]))
                   - jnp.exp(jnp.sum(p["lambda_q2"][e] * p["lambda_k2"][e])) + lam_init)
            scal = jnp.stack([lam, jnp.float32(1.0 - lam_init)]).reshape(1, 2).astype(F32)
            sw = p["subln_w"][e].reshape(1, -1)
            fz = _proj_f32(h, w_in, (3, 7, 8), tn, tm)
            qkv_b = _proj_bf16(h, w_in, (4, 5, 6), (1.0, float(LANES) ** -0.5, 1.0), tn, tm)
            ng = 2 * hb_heads
            w_g = jnp.pad(w_in[:, 9 * tn:9 * tn + ng], ((0, 0), (0, LANES - ng)))
            b_g = jnp.pad(p["b_gates_even"][e], (0, LANES - ng)).reshape(1, LANES)
            gates = _proj_bias(h, w_g, b_g, tm)
            if prompt:
                q1, q2 = _proj_q(h, w_in, 0, tn, tm, da, True)
                k_f, k_b = _proj_k(h, w_in, 1, tn, tm)
                v_f, v_t = _proj_vt(h, w_in, 2, tn, tm, heads, tk)
                oa = _attn_prompt(scal, q1, q2, k_b, v_t, fz, sw, heads, tq)
                dh = wb // hb_heads
                c0 = jnp.zeros((b, hb_heads, dh, dh), F32)
                n0 = jnp.zeros((b, hb_heads, dh), F32)
                m0 = jnp.zeros((b, 1, LANES), F32)
                L, cpb = _tile(t, LANES), 2
            else:
                q1, q2 = _proj_q(h, w_in, 0, tn, tm, da, False)
                kv = _proj_f32(h, w_in, (1, 2), tn, tm)
                k_f, v_f = kv[0], kv[1]
                ck = cache["k"][e].reshape(b, -1, heads * 2 * da)
                cv = cache["v"][e].reshape(b, -1, heads * 2 * da)
                oa = _attn_sample(scal, q1, q2, ck, cv, kv, fz, sw, heads, t)
                c0 = cache["c"][e]
                n0 = cache["n"][e]
                m0 = jnp.pad(cache["m"][e], ((0, 0), (0, LANES - hb_heads))).reshape(b, 1, LANES)
                L, cpb = t, 1
            hb, c_n, n_n, m_n = _mlstm(qkv_b, fz, gates, c0, n0, m0,
                                       p["mh_norm_w"][e].reshape(1, -1), b, L, cpb)
            k_list.append(k_f.reshape(b, t, heads, 2, da))
            v_list.append(v_f.reshape(b, t, heads, 2 * da))
            mc_list.append(c_n)
            mn_list.append(n_n)
            mm_list.append(m_n[:, 0, :hb_heads])
            x = _outproj([oa, hb], p["w_out_even"][e], x, rows(gate), tm)
        else:
            o = li // 2
            w_in = p["w_in_odd"][o]
            dr = w_in.shape[1] // 2
            xz = _proj_f32(h, w_in, (0, 1), dr, _tile(m, 512))
            wg = jnp.concatenate([p["rg_wa"][o], p["rg_wx"][o]], axis=-1).astype(BF16)
            if cache is None:
                buf0 = jnp.zeros((b, SUBLANES, dr), F32)
                h0 = jnp.zeros((b, 1, dr), F32)
            else:
                buf0 = jnp.pad(cache["conv"][o], ((0, 0), (SUBLANES - (CONV_W - 1), 0), (0, 0)))
                h0 = cache["h"][o].reshape(b, 1, dr)
            y, cs, hl = _rglru(xz, p["conv_w"][o], p["conv_b"][o].reshape(1, dr), wg,
                               p["rg_ba"][o].reshape(1, dr), p["rg_bx"][o].reshape(1, dr),
                               p["rg_lambda"][o].reshape(1, dr), buf0, h0, b, _tile(t, 256))
            conv_list.append(cs[:, SUBLANES - (CONV_W - 1):, :])
            hl_list.append(hl[:, 0, :])
            x = _outproj([y], p["w_out_odd"][o], x, rows(gate), tm)
    zero = jnp.zeros((1, d), F32)
    y = _norm(x, p["final_w"], zero, zero, F32, tm).reshape(b, t, d)
    ev = [jnp.stack(s) for s in (k_list, v_list, mc_list, mn_list, mm_list)]
    od = [jnp.stack(s) for s in (conv_list, hl_list)]
    return y, ev, od


def kernel(x_prompt, x_sample, c_prompt, c_sample, cache_k, cache_v, state_mlstm_c, state_mlstm_n, state_mlstm_m, state_conv, state_rglru, norm_w, w_ada, b_ada, w_in_even, b_gates_even, lambda_q1, lambda_k1, lambda_q2, lambda_k2, subln_w, mh_norm_w, w_out_even, w_in_odd, conv_w, conv_b, rg_wa, rg_ba, rg_wx, rg_bx, rg_lambda, w_out_odd, final_w):
    p = dict(norm_w=norm_w, w_in_even=w_in_even, b_gates_even=b_gates_even,
             lambda_q1=lambda_q1, lambda_k1=lambda_k1, lambda_q2=lambda_q2, lambda_k2=lambda_k2,
             subln_w=subln_w, mh_norm_w=mh_norm_w, w_out_even=w_out_even, w_in_odd=w_in_odd,
             conv_w=conv_w, conv_b=conv_b, rg_wa=rg_wa, rg_ba=rg_ba, rg_wx=rg_wx, rg_bx=rg_bx,
             rg_lambda=rg_lambda, w_out_odd=w_out_odd, final_w=final_w)
    depth, d = norm_w.shape
    bp, bs = c_prompt.shape[0], c_sample.shape[0]
    rpad = -(bp + bs) % SUBLANES
    c_all = jnp.pad(jnp.concatenate([c_prompt, c_sample], axis=0), ((0, rpad), (0, 0)))
    mod = _modulation(c_all, w_ada, b_ada)

    def mods_for(lo, hi):
        return [tuple(mod[li, lo:hi, i * d:(i + 1) * d] for i in range(3)) for li in range(depth)]

    y_p, ev_p, od_p = _trunk(x_prompt, mods_for(0, bp), p, None, True)
    cache = dict(k=cache_k, v=cache_v, c=state_mlstm_c, n=state_mlstm_n, m=state_mlstm_m,
                 conv=state_conv, h=state_rglru)
    y_s, ev_s, od_s = _trunk(x_sample, mods_for(bp, bp + bs), p, cache, False)
    return (y_p, y_s, *ev_p, *od_p, *ev_s, *od_s)
```

```python
import functools
import math

import jax
import jax.numpy as jnp
from jax import lax
from jax.experimental import pallas as pl
from jax.experimental.pallas import tpu as pltpu

F32 = jnp.float32
BF16 = jnp.bfloat16
EPS = 1e-6
CHUNK = 64
C_RG = 8.0
CONV_W = 4
LANES = 128
SUBLANES = 8
ONES_ROWS = 16
NT_DIMS = (((1,), (1,)), ((), ()))
TN_DIMS = (((0,), (0,)), ((), ()))
HIGHEST = lax.Precision.HIGHEST
VMEM_LIMIT = 56 * 1024 * 1024


def _params(sem):
    return pltpu.CompilerParams(dimension_semantics=sem, vmem_limit_bytes=VMEM_LIMIT)


def _silu(z):
    return z * jax.nn.sigmoid(z)


def _sigmoid_t(z):
    return 0.5 * jnp.tanh(0.5 * z) + 0.5


def _mod_kernel(c_ref, w_ref, b_ref, o_ref):
    sc = _silu(c_ref[...]).astype(BF16)
    o_ref[0] = jnp.dot(sc, w_ref[0].astype(BF16), preferred_element_type=F32) + b_ref[0]


def _modulation(c_all, w_ada, b_ada):
    depth, d, n = w_ada.shape
    r = c_all.shape[0]
    tn = 1024
    return pl.pallas_call(
        _mod_kernel,
        grid=(depth, n // tn),
        in_specs=[
            pl.BlockSpec((r, d), lambda l, j: (0, 0)),
            pl.BlockSpec((1, d, tn), lambda l, j: (l, 0, j)),
            pl.BlockSpec((1, 1, tn), lambda l, j: (l, 0, j)),
        ],
        out_specs=pl.BlockSpec((1, r, tn), lambda l, j: (l, 0, j)),
        out_shape=jax.ShapeDtypeStruct((depth, r, n), F32),
        compiler_params=_params(("arbitrary", "arbitrary")),
        name="adaln_mod",
    )(c_all, w_ada, b_ada.reshape(depth, 1, n))


def _norm_kernel(x_ref, nw_ref, sc_ref, sh_ref, o_ref):
    x = x_ref[...]
    ms = jnp.mean(x * x, axis=-1, keepdims=True)
    y = (x * lax.rsqrt(ms + EPS)) * nw_ref[...]
    o_ref[...] = (y * (1.0 + sc_ref[...]) + sh_ref[...]).astype(o_ref.dtype)


def _row_spec(arr, tm):
    if arr.shape[0] == 1:
        return pl.BlockSpec((1, arr.shape[1]), lambda i: (0, 0))
    return pl.BlockSpec((tm, arr.shape[1]), lambda i: (i, 0))


def _norm(x, nw, scale, shift, out_dtype, tm):
    m, d = x.shape
    return pl.pallas_call(
        _norm_kernel,
        grid=(m // tm,),
        in_specs=[
            pl.BlockSpec((tm, d), lambda i: (i, 0)),
            pl.BlockSpec((1, d), lambda i: (0, 0)),
            _row_spec(scale, tm),
            _row_spec(shift, tm),
        ],
        out_specs=pl.BlockSpec((tm, d), lambda i: (i, 0)),
        out_shape=jax.ShapeDtypeStruct((m, d), out_dtype),
        compiler_params=_params(("arbitrary",)),
        name="adaln_norm",
    )(x, nw.reshape(1, d), scale, shift)


def _proj_acc(h_ref, w_ref, wbf_ref):
    @pl.when(pl.program_id(1) == 0)
    def _():
        wbf_ref[...] = w_ref[...].astype(BF16)

    return jnp.dot(h_ref[...], wbf_ref[...], preferred_element_type=F32)


def _proj_f32_kernel(h_ref, w_ref, o_ref, wbf_ref):
    o_ref[...] = _proj_acc(h_ref, w_ref, wbf_ref)


def _proj_bias_kernel(h_ref, w_ref, b_ref, o_ref, wbf_ref):
    o_ref[...] = _proj_acc(h_ref, w_ref, wbf_ref) + b_ref[...]


def _proj_bf16_kernel(h_ref, w_ref, o_ref, wbf_ref, *, scales):
    acc = _proj_acc(h_ref, w_ref, wbf_ref)
    j = pl.program_id(0)
    s = jnp.float32(scales[0])
    for t in range(1, len(scales)):
        s = jnp.where(j == t, jnp.float32(scales[t]), s)
    o_ref[...] = (acc * s).astype(BF16)


def _proj_k_kernel(h_ref, w_ref, o_ref, obf_ref, wbf_ref):
    acc = _proj_acc(h_ref, w_ref, wbf_ref)
    o_ref[...] = acc
    obf_ref[...] = acc.astype(BF16)


def _proj_vt_kernel(h_ref, w_ref, o_ref, ot_ref, wbf_ref, *, tk):
    acc = _proj_acc(h_ref, w_ref, wbf_ref)
    o_ref[...] = acc
    at = acc.T.astype(BF16)
    nh, nt = ot_ref.shape[0], ot_ref.shape[1]
    va = ot_ref.shape[2] - ONES_ROWS
    ones = (lax.broadcasted_iota(jnp.int32, (ONES_ROWS, tk), 0) == 0).astype(BF16)
    for hh in range(nh):
        for t in range(nt):
            ot_ref[hh, t, 0:va, :] = at[hh * va:(hh + 1) * va, t * tk:(t + 1) * tk]
            ot_ref[hh, t, va:va + ONES_ROWS, :] = ones


def _q_split(acc, axis, da, qk_scale):
    idx = lax.broadcasted_iota(jnp.int32, acc.shape, axis)
    first = (idx % (2 * da)) < da
    a = acc * qk_scale
    zero = jnp.zeros_like(a)
    return jnp.where(first, a, zero).astype(BF16), jnp.where(first, zero, a).astype(BF16)


def _proj_q_kernel(h_ref, w_ref, o1_ref, o2_ref, wbf_ref, *, da, qk_scale):
    acc = _proj_acc(h_ref, w_ref, wbf_ref)
    o1_ref[...], o2_ref[...] = _q_split(acc, 1, da, qk_scale)


def _proj_qt_kernel(h_ref, w_ref, o1_ref, o2_ref, wbf_ref, *, da, qk_scale):
    acc = _proj_acc(h_ref, w_ref, wbf_ref)
    o1_ref[...], o2_ref[...] = _q_split(acc.T, 0, da, qk_scale)


def _col_index(cols, j):
    idx = jnp.int32(cols[0])
    for t in range(1, len(cols)):
        idx = jnp.where(j == t, jnp.int32(cols[t]), idx)
    return idx


def _proj_call(body, h, w, cols, tn, tm, out_shapes, out_specs, extra_in=(), extra_specs=(), name="proj"):
    m, d = h.shape
    nj = len(cols)
    return pl.pallas_call(
        body,
        grid=(nj, m // tm),
        in_specs=[
            pl.BlockSpec((tm, d), lambda j, i: (i, 0)),
            pl.BlockSpec((d, tn), lambda j, i: (0, _col_index(cols, j))),
            *extra_specs,
        ],
        out_specs=out_specs,
        out_shape=out_shapes,
        scratch_shapes=[pltpu.VMEM((d, tn), BF16)],
        compiler_params=_params(("arbitrary", "arbitrary")),
        name=name,
    )(h, w, *extra_in)


def _proj_f32(h, w, cols, tn, tm):
    m = h.shape[0]
    return _proj_call(
        _proj_f32_kernel, h, w, cols, tn, tm,
        jax.ShapeDtypeStruct((len(cols), m, tn), F32),
        pl.BlockSpec((None, tm, tn), lambda j, i: (j, i, 0)), name="proj_f32")


def _proj_bf16(h, w, cols, scales, tn, tm):
    m = h.shape[0]
    return _proj_call(
        functools.partial(_proj_bf16_kernel, scales=tuple(scales)), h, w, cols, tn, tm,
        jax.ShapeDtypeStruct((len(cols), m, tn), BF16),
        pl.BlockSpec((None, tm, tn), lambda j, i: (j, i, 0)), name="proj_bf16")


def _proj_bias(h, w_small, bias, tm):
    m = h.shape[0]
    tn = w_small.shape[1]
    return _proj_call(
        _proj_bias_kernel, h, w_small, (0,), tn, tm,
        jax.ShapeDtypeStruct((m, tn), F32),
        pl.BlockSpec((tm, tn), lambda j, i: (i, 0)),
        extra_in=(bias,), extra_specs=(pl.BlockSpec((1, tn), lambda j, i: (0, 0)),), name="proj_gates")


def _proj_k(h, w, col, tn, tm):
    m = h.shape[0]
    return _proj_call(
        _proj_k_kernel, h, w, (col,), tn, tm,
        (jax.ShapeDtypeStruct((m, tn), F32), jax.ShapeDtypeStruct((m, tn), BF16)),
        (pl.BlockSpec((tm, tn), lambda j, i: (i, 0)), pl.BlockSpec((tm, tn), lambda j, i: (i, 0))),
        name="proj_k")


def _proj_vt(h, w, col, tn, tm, heads, tk):
    m = h.shape[0]
    va = tn // heads
    return _proj_call(
        functools.partial(_proj_vt_kernel, tk=tk), h, w, (col,), tn, tm,
        (jax.ShapeDtypeStruct((m, tn), F32),
         jax.ShapeDtypeStruct((heads, m // tk, va + ONES_ROWS, tk), BF16)),
        (pl.BlockSpec((tm, tn), lambda j, i: (i, 0)),
         pl.BlockSpec((heads, tm // tk, va + ONES_ROWS, tk), lambda j, i: (0, i, 0, 0))),
        name="proj_vt")


def _proj_q(h, w, col, tn, tm, da, transposed):
    m = h.shape[0]
    qk_scale = float(da) ** -0.5
    if transposed:
        qk_scale *= math.log2(math.e)
        body = functools.partial(_proj_qt_kernel, da=da, qk_scale=qk_scale)
        shape = jax.ShapeDtypeStruct((tn, m), BF16)
        spec = pl.BlockSpec((tn, tm), lambda j, i: (0, i))
    else:
        body = functools.partial(_proj_q_kernel, da=da, qk_scale=qk_scale)
        shape = jax.ShapeDtypeStruct((m, tn), BF16)
        spec = pl.BlockSpec((tm, tn), lambda j, i: (i, 0))
    return _proj_call(body, h, w, (col,), tn, tm, (shape, shape), (spec, spec), name="proj_q")


def _attn_finish(o, z, sw, post):
    ms = jnp.mean(o * o, axis=-1, keepdims=True)
    on = (o * lax.rsqrt(ms + EPS)) * sw
    return (on * post) * _silu(z.astype(F32))


def _attn_prompt_kernel(sc_ref, q1_ref, q2_ref, k_ref, vt_ref, za_ref, sw_ref, o_ref,
                        m_ref, acc_ref, sa_ref, sb_ref, cma_ref, cmb_ref, *, tq, tk):
    i = pl.program_id(1)
    q_refs = (q1_ref, q2_ref)
    m_ref[...] = jnp.full(m_ref.shape, -jnp.inf, F32)
    acc_ref[...] = jnp.zeros(acc_ref.shape, F32)
    r = tq // tk
    va = o_ref.shape[1]

    def scores(j, c, lo):
        kt = k_ref[pl.ds(pl.multiple_of(j * tk, tk), tk), :]
        return jnp.dot(kt, q_refs[c][:, lo:], preferred_element_type=F32)

    def update(j, c, s, smax, lo):
        vt = vt_ref[j]
        m_old = m_ref[c, :, lo:]
        m_new = jnp.maximum(m_old, smax)
        alpha = jnp.exp2(m_old - m_new)
        p = jnp.exp2(s - m_new)
        acc_ref[c, :, lo:] = (alpha * acc_ref[c, :, lo:]
                              + jnp.dot(vt, p.astype(BF16), preferred_element_type=F32))
        m_ref[c, :, lo:] = m_new

    def fill(j, buf, lo=0):
        s_ref, cm_ref = buf
        for c in range(2):
            s = scores(j, c, lo)
            s_ref[c, :, lo:] = s
            cm_ref[c, :, lo:] = jnp.max(s, axis=0, keepdims=True)

    def drain(j, buf, lo=0, masked=False):
        s_ref, cm_ref = buf
        for c in range(2):
            s = s_ref[c, :, lo:]
            if masked:
                kc = lax.broadcasted_iota(jnp.int32, s.shape, 0) // CHUNK
                qc = lax.broadcasted_iota(jnp.int32, s.shape, 1) // CHUNK
                s = jnp.where(kc <= qc, s, -jnp.inf)
                smax = jnp.max(s, axis=0, keepdims=True)
            else:
                smax = cm_ref[c, :, lo:]
            update(j, c, s, smax, lo)

    n = i * r
    bufs = ((sa_ref, cma_ref), (sb_ref, cmb_ref))
    fill(0, bufs[0])

    def body(t, carry):
        j = 2 * t
        fill(j + 1, bufs[1])
        drain(j, bufs[0])
        fill(j + 2, bufs[0])
        drain(j + 1, bufs[1])
        return carry

    lax.fori_loop(0, n // 2, body, 0)
    for dd in range(r):
        if dd + 1 < r:
            fill(n + dd + 1, bufs[(dd + 1) % 2], (dd + 1) * tk)
        drain(n + dd, bufs[dd % 2], dd * tk, masked=True)

    lam = sc_ref[0, 0]
    post = sc_ref[0, 1]
    o = (acc_ref[0, 0:va, :] / acc_ref[0, va:va + 1, :]
         - lam * (acc_ref[1, 0:va, :] / acc_ref[1, va:va + 1, :]))
    o_ref[...] = _attn_finish(o.T, za_ref[...], sw_ref[...], post).astype(o_ref.dtype)


def _attn_prompt(scal, q1t, q2t, kbf, vt, fz, sw, heads, tq):
    dq, m = q1t.shape
    hd = dq // heads
    tk = vt.shape[3]
    hv = vt.shape[2]
    assert tq % (2 * tk) == 0 and m % tq == 0
    nk = m // tk
    return pl.pallas_call(
        functools.partial(_attn_prompt_kernel, tq=tq, tk=tk),
        grid=(heads, m // tq),
        in_specs=[
            pl.BlockSpec(memory_space=pltpu.SMEM),
            pl.BlockSpec((hd, tq), lambda h, i: (h, i)),
            pl.BlockSpec((hd, tq), lambda h, i: (h, i)),
            pl.BlockSpec((m, hd), lambda h, i: (0, h)),
            pl.BlockSpec((None, nk, hv, tk), lambda h, i: (h, 0, 0, 0)),
            pl.BlockSpec((None, tq, hd), lambda h, i: (0, i, h)),
            pl.BlockSpec((1, hd), lambda h, i: (0, 0)),
        ],
        out_specs=pl.BlockSpec((tq, hd), lambda h, i: (i, h)),
        out_shape=jax.ShapeDtypeStruct((m, dq), BF16),
        scratch_shapes=[
            pltpu.VMEM((2, 1, tq), F32),
            pltpu.VMEM((2, hv, tq), F32),
            pltpu.VMEM((2, tk, tq), F32),
            pltpu.VMEM((2, tk, tq), F32),
            pltpu.VMEM((2, 1, tq), F32),
            pltpu.VMEM((2, 1, tq), F32),
        ],
        compiler_params=_params(("arbitrary", "arbitrary")),
        name="diff_attn_prompt",
    )(scal, q1t, q2t, kbf, vt, fz, sw)


def _attn_sample_kernel(sc_ref, q1_ref, q2_ref, ck_ref, cv_ref, kn_ref, vn_ref, za_ref, sw_ref, o_ref,
                        *, heads):
    hd = o_ref.shape[1] // heads
    lam = sc_ref[0, 0]
    post = sc_ref[0, 1]
    for h in range(heads):
        hs = slice(h * hd, (h + 1) * hd)
        ck = ck_ref[:, hs].astype(BF16)
        cv = cv_ref[:, hs].astype(BF16)
        kn = kn_ref[:, hs].astype(BF16)
        vn = vn_ref[:, hs].astype(BF16)
        outs = []
        for q_ref in (q1_ref, q2_ref):
            q = q_ref[:, hs]
            sc = lax.dot_general(q, ck, NT_DIMS, preferred_element_type=F32)
            sn = lax.dot_general(q, kn, NT_DIMS, preferred_element_type=F32)
            mx = jnp.maximum(jnp.max(sc, axis=-1, keepdims=True), jnp.max(sn, axis=-1, keepdims=True))
            pc = jnp.exp(sc - mx)
            pn = jnp.exp(sn - mx)
            l = jnp.sum(pc, axis=-1, keepdims=True) + jnp.sum(pn, axis=-1, keepdims=True)
            acc = (jnp.dot(pc.astype(BF16), cv, preferred_element_type=F32)
                   + jnp.dot(pn.astype(BF16), vn, preferred_element_type=F32))
            outs.append(acc / l)
        o = outs[0] - lam * outs[1]
        o_ref[:, hs] = _attn_finish(o, za_ref[:, hs], sw_ref[...], post).astype(o_ref.dtype)


def _attn_sample(scal, q1, q2, ck, cv, kv, fz, sw, heads, t):
    m, dq = q1.shape
    hd = dq // heads
    b, p, _ = ck.shape
    return pl.pallas_call(
        functools.partial(_attn_sample_kernel, heads=heads),
        grid=(b,),
        in_specs=[
            pl.BlockSpec(memory_space=pltpu.SMEM),
            pl.BlockSpec((t, dq), lambda bb: (bb, 0)),
            pl.BlockSpec((t, dq), lambda bb: (bb, 0)),
            pl.BlockSpec((None, p, dq), lambda bb: (bb, 0, 0)),
            pl.BlockSpec((None, p, dq), lambda bb: (bb, 0, 0)),
            pl.BlockSpec((None, t, dq), lambda bb: (0, bb, 0)),
            pl.BlockSpec((None, t, dq), lambda bb: (1, bb, 0)),
            pl.BlockSpec((None, t, dq), lambda bb: (0, bb, 0)),
            pl.BlockSpec((1, hd), lambda bb: (0, 0)),
        ],
        out_specs=pl.BlockSpec((t, dq), lambda bb: (bb, 0)),
        out_shape=jax.ShapeDtypeStruct((m, dq), BF16),
        compiler_params=_params(("arbitrary",)),
        name="diff_attn_sample",
    )(scal, q1, q2, ck, cv, kv, kv, fz, sw)


def _mlstm_kernel(q_ref, k_ref, v_ref, ob_ref, zb_ref, g_ref, c0_ref, n0_ref, m0_ref, w_ref,
                  hb_ref, c_out, n_out, m_out, ct_scr, m_scr, *, L, cpb, H, DH):
    t = pl.program_id(1)

    @pl.when(t == 0)
    def _():
        for h in range(H):
            ct_scr[h, 0:DH, :] = c0_ref[0, h].T
            ct_scr[h, DH:DH + SUBLANES, :] = jnp.zeros((SUBLANES, DH), F32)
            ct_scr[h, DH:DH + 1, :] = n0_ref[0, h:h + 1, :]
        m_scr[...] = m0_ref[0]

    src = lax.broadcasted_iota(jnp.int32, (L, L), 0)
    tgt = lax.broadcasted_iota(jnp.int32, (L, L), 1)
    visible = src <= tgt
    tril = (tgt <= src).astype(F32)
    sel = (lax.broadcasted_iota(jnp.int32, (2 * H, LANES), 0)
           == lax.broadcasted_iota(jnp.int32, (2 * H, LANES), 1)).astype(F32)
    lane = lax.broadcasted_iota(jnp.int32, (1, LANES), 1)

    for cc in range(cpb):
        r0 = cc * L
        g = g_ref[r0:r0 + L, :]
        lf = jax.nn.log_sigmoid(g)
        bcum = jnp.dot(tril, lf, precision=HIGHEST, preferred_element_type=F32)
        x = jnp.where(lane < H, g, bcum)
        xt = lax.dot_general(sel, x, NT_DIMS, precision=HIGHEST, preferred_element_type=F32)
        csx = g - pltpu.roll(bcum, LANES - H, 1)
        m_all = m_scr[...]
        m_next = m_all
        for h in range(H):
            b_row = xt[H + h:H + h + 1, :]
            cs = jnp.broadcast_to(csx[:, h:h + 1], (L, DH))
            m0 = m_all[:, h:h + 1]
            dmat = jnp.where(visible, b_row + cs[:, 0:L], -jnp.inf)
            m_row = jnp.maximum(b_row + m0, jnp.max(dmat, axis=0, keepdims=True))
            wgt = jnp.exp(dmat - m_row)
            inter = jnp.exp(b_row + m0 - m_row)
            hs = slice(h * DH, (h + 1) * DH)
            qh = q_ref[r0:r0 + L, hs]
            kh = k_ref[r0:r0 + L, hs]
            vh = v_ref[r0:r0 + L, hs]
            st = lax.dot_general(kh, qh, NT_DIMS, preferred_element_type=F32) * wgt
            ctn = ct_scr[h]
            carried = lax.dot_general(ctn.astype(BF16), qh, NT_DIMS,
                                      preferred_element_type=F32)
            num = (lax.dot_general(vh, st.astype(BF16), TN_DIMS, preferred_element_type=F32)
                   + inter * carried[0:DH, :])
            den = jnp.sum(st, axis=0, keepdims=True) + inter * carried[DH:DH + 1, :]
            denom = jnp.maximum(jnp.abs(den), jnp.exp(-m_row))
            hh = (num / denom).T
            hh = jax.nn.sigmoid(ob_ref[r0:r0 + L, hs].astype(F32)) * hh
            ms = jnp.mean(hh * hh, axis=-1, keepdims=True)
            hn = (hh * lax.rsqrt(ms + EPS)) * w_ref[:, hs]
            hb_ref[r0:r0 + L, hs] = (hn * _silu(zb_ref[r0:r0 + L, hs].astype(F32))).astype(hb_ref.dtype)
            b_last = b_row[:, L - 1:L]
            m_last = m_row[:, L - 1:L]
            ws = jnp.exp(cs + (b_last - m_last))
            decay = jnp.exp(b_last + m0 - m_last)
            wv = (ws * vh.astype(F32)).astype(BF16)
            ct_scr[h, 0:DH, :] = (decay * ctn[0:DH, :]
                                  + lax.dot_general(wv, kh, TN_DIMS, preferred_element_type=F32))
            ct_scr[h, DH:DH + 1, :] = (decay * ctn[DH:DH + 1, :]
                                       + jnp.sum(ws * kh.astype(F32), axis=0, keepdims=True))
            m_next = jnp.where(lane == h, m_last, m_next)
        m_scr[...] = m_next

    @pl.when(t == pl.num_programs(1) - 1)
    def _():
        for h in range(H):
            c_out[0, h] = ct_scr[h, 0:DH, :].T
            n_out[0, h:h + 1, :] = ct_scr[h, DH:DH + 1, :]
        m_out[0] = m_scr[...]


def _mlstm(qkv, fz, gates, c0, n0, m0, mh_w, batch, L, cpb):
    _, m, wb = qkv.shape
    _, H, DH, _ = c0.shape
    rows = L * cpb
    nt = m // batch // rows
    blk = lambda which: pl.BlockSpec((None, rows, wb), lambda b, t: (which, b * nt + t, 0))
    return pl.pallas_call(
        functools.partial(_mlstm_kernel, L=L, cpb=cpb, H=H, DH=DH),
        grid=(batch, nt),
        in_specs=[
            blk(0), blk(1), blk(2),
            blk(1), blk(2),
            pl.BlockSpec((rows, LANES), lambda b, t: (b * nt + t, 0)),
            pl.BlockSpec((1, H, DH, DH), lambda b, t: (b, 0, 0, 0)),
            pl.BlockSpec((1, H, DH), lambda b, t: (b, 0, 0)),
            pl.BlockSpec((1, 1, LANES), lambda b, t: (b, 0, 0)),
            pl.BlockSpec((1, wb), lambda b, t: (0, 0)),
        ],
        out_specs=(
            pl.BlockSpec((rows, wb), lambda b, t: (b * nt + t, 0)),
            pl.BlockSpec((1, H, DH, DH), lambda b, t: (b, 0, 0, 0)),
            pl.BlockSpec((1, H, DH), lambda b, t: (b, 0, 0)),
            pl.BlockSpec((1, 1, LANES), lambda b, t: (b, 0, 0)),
        ),
        out_shape=(
            jax.ShapeDtypeStruct((m, wb), BF16),
            jax.ShapeDtypeStruct((batch, H, DH, DH), F32),
            jax.ShapeDtypeStruct((batch, H, DH), F32),
            jax.ShapeDtypeStruct((batch, 1, LANES), F32),
        ),
        scratch_shapes=[
            pltpu.VMEM((H, DH + SUBLANES, DH), F32),
            pltpu.VMEM((1, LANES), F32),
        ],
        compiler_params=_params(("arbitrary", "arbitrary")),
        name="mlstm_chunks",
    )(qkv, qkv, qkv, fz, fz, gates, c0, n0, m0, mh_w)


def _outproj_kernel(*refs, n_in, emit_x):
    a_refs = refs[:n_in]
    w_ref, x_ref, g_ref, nw_ref, sc_ref, sh_ref = refs[n_in:n_in + 6]
    outs = refs[n_in + 6:]
    wbf_ref = outs[-1]
    h_ref = outs[-2]

    @pl.when(pl.program_id(0) == 0)
    def _():
        wbf_ref[...] = w_ref[...].astype(BF16)

    acc = None
    off = 0
    for a_ref in a_refs:
        kk = a_ref.shape[1]
        part = jnp.dot(a_ref[...], wbf_ref[off:off + kk, :], preferred_element_type=F32)
        acc = part if acc is None else acc + part
        off += kk
    xn = x_ref[...] + g_ref[...] * acc
    if emit_x:
        outs[0][...] = xn
    ms = jnp.mean(xn * xn, axis=-1, keepdims=True)
    y = (xn * lax.rsqrt(ms + EPS)) * nw_ref[...]
    h_ref[...] = (y * (1.0 + sc_ref[...]) + sh_ref[...]).astype(h_ref.dtype)


def _outproj(acts, w, x, gate, nw, scale, shift, h_dtype, emit_x, tm):
    m, d = x.shape
    kt = w.shape[0]
    row = pl.BlockSpec((tm, d), lambda i: (i, 0))
    out_shapes = [jax.ShapeDtypeStruct((m, d), h_dtype)]
    out_specs = [row]
    if emit_x:
        out_shapes.insert(0, jax.ShapeDtypeStruct((m, d), F32))
        out_specs.insert(0, row)
    res = pl.pallas_call(
        functools.partial(_outproj_kernel, n_in=len(acts), emit_x=emit_x),
        grid=(m // tm,),
        in_specs=[
            *[pl.BlockSpec((tm, a.shape[1]), lambda i: (i, 0)) for a in acts],
            pl.BlockSpec((kt, d), lambda i: (0, 0)),
            row,
            _row_spec(gate, tm),
            pl.BlockSpec((1, d), lambda i: (0, 0)),
            _row_spec(scale, tm),
            _row_spec(shift, tm),
        ],
        out_specs=out_specs,
        out_shape=out_shapes,
        scratch_shapes=[pltpu.VMEM((kt, d), BF16)],
        compiler_params=_params(("arbitrary",)),
        name="outproj_residual",
    )(*acts, w, x, gate, nw.reshape(1, d), scale, shift)
    return (res[0], res[1]) if emit_x else (None, res[0])


def _rglru_kernel(xr_ref, zr_ref, cw_ref, cb_ref, wg_ref, ba_ref, bx_ref, lam_ref, buf0_ref, h0_ref,
                  y_ref, cs_ref, hl_ref, xbuf, a_scr, u_scr, hcar, *, tt, nblk, bw):
    t = pl.program_id(1)
    pad = SUBLANES

    @pl.when(t == 0)
    def _():
        xbuf[0:pad, :] = buf0_ref[0]
        hcar[...] = h0_ref[0]

    xbuf[pad:pad + tt, :] = xr_ref[...]
    xc = cb_ref[...] + cw_ref[CONV_W - 1:CONV_W, :] * xbuf[pad:pad + tt, :]
    for j in range(1, CONV_W):
        xc = xc + cw_ref[CONV_W - 1 - j:CONV_W - j, :] * xbuf[pad - j:pad - j + tt, :]

    neg_c = -C_RG * jax.nn.softplus(-lam_ref[...])
    for hb in range(nblk):
        cs = slice(hb * bw, (hb + 1) * bw)
        xb = xc[:, cs]
        gg = jnp.dot(xb.astype(BF16), wg_ref[hb], preferred_element_type=F32)
        r = _sigmoid_t(gg[:, :bw] + ba_ref[:, cs])
        ig = _sigmoid_t(gg[:, bw:] + bx_ref[:, cs])
        log_a = neg_c[:, cs] * r
        th = jnp.tanh(log_a)
        one_minus_a2 = (-2.0 * th) / (1.0 - th)
        a_scr[:, cs] = jnp.exp(log_a)
        u_scr[:, cs] = jnp.sqrt(one_minus_a2) * (ig * xb)

    rowi = lax.broadcasted_iota(jnp.int32, (SUBLANES, a_scr.shape[1]), 0)

    def group(gi, hc):
        r0 = pl.multiple_of(gi * SUBLANES, SUBLANES)
        a = a_scr[pl.ds(r0, SUBLANES), :]
        u = u_scr[pl.ds(r0, SUBLANES), :]
        for dd in (1, 2, 4):
            ok = rowi >= dd
            u = jnp.where(ok, a * pltpu.roll(u, dd, 0) + u, u)
            a = jnp.where(ok, a * pltpu.roll(a, dd, 0), a)
        hgrp = u + a * hc
        u_scr[pl.ds(r0, SUBLANES), :] = hgrp
        return hgrp[SUBLANES - 1:SUBLANES, :]

    hc = lax.fori_loop(0, tt // SUBLANES, group, hcar[...])
    hcar[...] = hc
    zr = zr_ref[...]
    y_ref[...] = (u_scr[...] * (zr * _sigmoid_t(zr))).astype(y_ref.dtype)
    xbuf[0:pad, :] = xbuf[tt:tt + pad, :]

    @pl.when(t == pl.num_programs(1) - 1)
    def _():
        cs_ref[0] = xbuf[0:pad, :]
        hl_ref[0] = hc


def _rglru(xz, cw, cb, wg, ba, bx, lam, buf0, h0, batch, tt):
    _, m, dr = xz.shape
    nblk, bw, _ = wg.shape
    nt = m // batch // tt
    vec = lambda: pl.BlockSpec((1, dr), lambda b, t: (0, 0))
    return pl.pallas_call(
        functools.partial(_rglru_kernel, tt=tt, nblk=nblk, bw=bw),
        grid=(batch, nt),
        in_specs=[
            pl.BlockSpec((None, tt, dr), lambda b, t: (0, b * nt + t, 0)),
            pl.BlockSpec((None, tt, dr), lambda b, t: (1, b * nt + t, 0)),
            pl.BlockSpec((CONV_W, dr), lambda b, t: (0, 0)),
            vec(),
            pl.BlockSpec((nblk, bw, 2 * bw), lambda b, t: (0, 0, 0)),
            vec(), vec(), vec(),
            pl.BlockSpec((1, SUBLANES, dr), lambda b, t: (b, 0, 0)),
            pl.BlockSpec((1, 1, dr), lambda b, t: (b, 0, 0)),
        ],
        out_specs=(
            pl.BlockSpec((tt, dr), lambda b, t: (b * nt + t, 0)),
            pl.BlockSpec((1, SUBLANES, dr), lambda b, t: (b, 0, 0)),
            pl.BlockSpec((1, 1, dr), lambda b, t: (b, 0, 0)),
        ),
        out_shape=(
            jax.ShapeDtypeStruct((m, dr), BF16),
            jax.ShapeDtypeStruct((batch, SUBLANES, dr), F32),
            jax.ShapeDtypeStruct((batch, 1, dr), F32),
        ),
        scratch_shapes=[
            pltpu.VMEM((tt + SUBLANES, dr), F32),
            pltpu.VMEM((tt, dr), F32),
            pltpu.VMEM((tt, dr), F32),
            pltpu.VMEM((1, dr), F32),
        ],
        compiler_params=_params(("arbitrary", "arbitrary")),
        name="rglru_block",
    )(xz, xz, cw, cb, wg, ba, bx, lam, buf0, h0)


def _tile(m, pref):
    t = min(m, pref)
    assert m % t == 0
    return t


def _trunk(x3, mods, p, cache, prompt):
    b, t, d = x3.shape
    m = b * t
    depth = p["norm_w"].shape[0]
    x = x3.reshape(m, d)
    tm = _tile(m, 1024)
    va = p["subln_w"].shape[1]
    da = va // 2
    wb = p["mh_norm_w"].shape[1]
    tn = p["w_out_even"].shape[1] - wb
    assert tn == wb
    heads = tn // va
    hb_heads = p["b_gates_even"].shape[1] // 2
    tk = 512
    tq = _tile(m, 1024)
    k_list, v_list, mc_list, mn_list, mm_list, conv_list, hl_list = [], [], [], [], [], [], []

    def rows(v):
        if b == 1:
            return v
        return jnp.broadcast_to(v[:, None, :], (b, t, d)).reshape(m, d)

    zero = jnp.zeros((1, d), F32)

    def close_layer(li, acts, w_out, x, gate):
        if li + 1 < depth:
            nshift, nscale, _ = mods[li + 1]
            return _outproj(acts, w_out, x, rows(gate), p["norm_w"][li + 1], rows(nscale), rows(nshift),
                            BF16, True, tm)
        return _outproj(acts, w_out, x, rows(gate), p["final_w"], zero, zero, F32, False, tm)

    shift, scale, _ = mods[0]
    h = _norm(x, p["norm_w"][0], rows(scale), rows(shift), BF16, tm)
    for li in range(depth):
        gate = mods[li][2]
        if li % 2 == 0:
            e = li // 2
            w_in = p["w_in_even"][e]
            lam_init = 0.8 - 0.6 * math.exp(-0.3 * li)
            lam = (jnp.exp(jnp.sum(p["lambda_q1"][e] * p["lambda_k1"][e]))
                   - jnp.exp(jnp.sum(p["lambda_q2"][e] * p["lambda_k2"][e])) + lam_init)
            scal = jnp.stack([lam, jnp.float32(1.0 - lam_init)]).reshape(1, 2).astype(F32)
            sw = p["subln_w"][e].reshape(1, -1)
            fz = _proj_bf16(h, w_in, (3, 7, 8), (1.0, 1.0, 1.0), tn, tm)
            qkv_b = _proj_bf16(h, w_in, (4, 5, 6), (1.0, float(LANES) ** -0.5, 1.0), tn, tm)
            ng = 2 * hb_heads
            w_g = jnp.pad(w_in[:, 9 * tn:9 * tn + ng], ((0, 0), (0, LANES - ng)))
            b_g = jnp.pad(p["b_gates_even"][e], (0, LANES - ng)).reshape(1, LANES)
            gates = _proj_bias(h, w_g, b_g, tm)
            if prompt:
                q1, q2 = _proj_q(h, w_in, 0, tn, tm, da, True)
                k_f, k_b = _proj_k(h, w_in, 1, tn, tm)
                v_f, v_t = _proj_vt(h, w_in, 2, tn, tm, heads, tk)
                oa = _attn_prompt(scal, q1, q2, k_b, v_t, fz, sw, heads, tq)
                dh = wb // hb_heads
                c0 = jnp.zeros((b, hb_heads, dh, dh), F32)
                n0 = jnp.zeros((b, hb_heads, dh), F32)
                m0 = jnp.zeros((b, 1, LANES), F32)
                L, cpb = _tile(t, LANES), 4
            else:
                q1, q2 = _proj_q(h, w_in, 0, tn, tm, da, False)
                kv = _proj_f32(h, w_in, (1, 2), tn, tm)
                k_f, v_f = kv[0], kv[1]
                ck = cache["k"][e].reshape(b, -1, heads * 2 * da)
                cv = cache["v"][e].reshape(b, -1, heads * 2 * da)
                oa = _attn_sample(scal, q1, q2, ck, cv, kv, fz, sw, heads, t)
                c0 = cache["c"][e]
                n0 = cache["n"][e]
                m0 = jnp.pad(cache["m"][e], ((0, 0), (0, LANES - hb_heads))).reshape(b, 1, LANES)
                L, cpb = t, 1
            hb, c_n, n_n, m_n = _mlstm(qkv_b, fz, gates, c0, n0, m0,
                                       p["mh_norm_w"][e].reshape(1, -1), b, L, cpb)
            k_list.append(k_f.reshape(b, t, heads, 2, da))
            v_list.append(v_f.reshape(b, t, heads, 2 * da))
            mc_list.append(c_n)
            mn_list.append(n_n)
            mm_list.append(m_n[:, 0, :hb_heads])
            x, h = close_layer(li, [oa, hb], p["w_out_even"][e], x, gate)
        else:
            o = li // 2
            w_in = p["w_in_odd"][o]
            dr = w_in.shape[1] // 2
            xz = _proj_f32(h, w_in, (0, 1), dr, _tile(m, 512))
            wg = jnp.concatenate([p["rg_wa"][o], p["rg_wx"][o]], axis=-1).astype(BF16)
            if cache is None:
                buf0 = jnp.zeros((b, SUBLANES, dr), F32)
                h0 = jnp.zeros((b, 1, dr), F32)
            else:
                buf0 = jnp.pad(cache["conv"][o], ((0, 0), (SUBLANES - (CONV_W - 1), 0), (0, 0)))
                h0 = cache["h"][o].reshape(b, 1, dr)
            y, cs, hl = _rglru(xz, p["conv_w"][o], p["conv_b"][o].reshape(1, dr), wg,
                               p["rg_ba"][o].reshape(1, dr), p["rg_bx"][o].reshape(1, dr),
                               p["rg_lambda"][o].reshape(1, dr), buf0, h0, b, _tile(t, 256))
            conv_list.append(cs[:, SUBLANES - (CONV_W - 1):, :])
            hl_list.append(hl[:, 0, :])
            x, h = close_layer(li, [y], p["w_out_odd"][o], x, gate)
    y = h.reshape(b, t, d)
    ev = [jnp.stack(s) for s in (k_list, v_list, mc_list, mn_list, mm_list)]
    od = [jnp.stack(s) for s in (conv_list, hl_list)]
    return y, ev, od


def kernel(x_prompt, x_sample, c_prompt, c_sample, cache_k, cache_v, state_mlstm_c, state_mlstm_n, state_mlstm_m, state_conv, state_rglru, norm_w, w_ada, b_ada, w_in_even, b_gates_even, lambda_q1, lambda_k1, lambda_q2, lambda_k2, subln_w, mh_norm_w, w_out_even, w_in_odd, conv_w, conv_b, rg_wa, rg_ba, rg_wx, rg_bx, rg_lambda, w_out_odd, final_w):
    p = dict(norm_w=norm_w, w_in_even=w_in_even, b_gates_even=b_gates_even,
             lambda_q1=lambda_q1, lambda_k1=lambda_k1, lambda_q2=lambda_q2, lambda_k2=lambda_k2,
             subln_w=subln_w, mh_norm_w=mh_norm_w, w_out_even=w_out_even, w_in_odd=w_in_odd,
             conv_w=conv_w, conv_b=conv_b, rg_wa=rg_wa, rg_ba=rg_ba, rg_wx=rg_wx, rg_bx=rg_bx,
             rg_lambda=rg_lambda, w_out_odd=w_out_odd, final_w=final_w)
    depth, d = norm_w.shape
    bp, bs = c_prompt.shape[0], c_sample.shape[0]
    rpad = -(bp + bs) % SUBLANES
    c_all = jnp.pad(jnp.concatenate([c_prompt, c_sample], axis=0), ((0, rpad), (0, 0)))
    mod = _modulation(c_all, w_ada, b_ada)

    def mods_for(lo, hi):
        return [tuple(mod[li, lo:hi, i * d:(i + 1) * d] for i in range(3)) for li in range(depth)]

    y_p, ev_p, od_p = _trunk(x_prompt, mods_for(0, bp), p, None, True)
    cache = dict(k=cache_k, v=cache_v, c=state_mlstm_c, n=state_mlstm_n, m=state_mlstm_m,
                 conv=state_conv, h=state_rglru)
    y_s, ev_s, od_s = _trunk(x_sample, mods_for(bp, bp + bs), p, cache, False)
    return (y_p, y_s, *ev_p, *od_p, *ev_s, *od_s)
```

```python
import functools
import math

import jax
import jax.numpy as jnp
from jax import lax
from jax.experimental import pallas as pl
from jax.experimental.pallas import tpu as pltpu

F32 = jnp.float32
BF16 = jnp.bfloat16
EPS = 1e-6
CHUNK = 64
C_RG = 8.0
CONV_W = 4
LANES = 128
SUBLANES = 8
ONES_ROWS = 16
NT_DIMS = (((1,), (1,)), ((), ()))
TN_DIMS = (((0,), (0,)), ((), ()))
HIGHEST = lax.Precision.HIGHEST
VMEM_LIMIT = 56 * 1024 * 1024


def _params(sem):
    return pltpu.CompilerParams(dimension_semantics=sem, vmem_limit_bytes=VMEM_LIMIT)


def _silu(z):
    return z * jax.nn.sigmoid(z)


def _sigmoid_t(z):
    return 0.5 * jnp.tanh(0.5 * z) + 0.5


def _mod_kernel(c_ref, w_ref, b_ref, o_ref):
    sc = _silu(c_ref[...]).astype(BF16)
    o_ref[0] = jnp.dot(sc, w_ref[0].astype(BF16), preferred_element_type=F32) + b_ref[0]


def _modulation(c_all, w_ada, b_ada):
    depth, d, n = w_ada.shape
    r = c_all.shape[0]
    tn = 1024
    return pl.pallas_call(
        _mod_kernel,
        grid=(depth, n // tn),
        in_specs=[
            pl.BlockSpec((r, d), lambda l, j: (0, 0)),
            pl.BlockSpec((1, d, tn), lambda l, j: (l, 0, j)),
            pl.BlockSpec((1, 1, tn), lambda l, j: (l, 0, j)),
        ],
        out_specs=pl.BlockSpec((1, r, tn), lambda l, j: (l, 0, j)),
        out_shape=jax.ShapeDtypeStruct((depth, r, n), F32),
        compiler_params=_params(("arbitrary", "arbitrary")),
        name="adaln_mod",
    )(c_all, w_ada, b_ada.reshape(depth, 1, n))


def _norm_kernel(x_ref, nw_ref, sc_ref, sh_ref, o_ref):
    x = x_ref[...]
    ms = jnp.mean(x * x, axis=-1, keepdims=True)
    y = (x * lax.rsqrt(ms + EPS)) * nw_ref[...]
    o_ref[...] = (y * (1.0 + sc_ref[...]) + sh_ref[...]).astype(o_ref.dtype)


def _row_spec(arr, tm):
    if arr.shape[0] == 1:
        return pl.BlockSpec((1, arr.shape[1]), lambda i: (0, 0))
    return pl.BlockSpec((tm, arr.shape[1]), lambda i: (i, 0))


def _norm(x, nw, scale, shift, out_dtype, tm):
    m, d = x.shape
    return pl.pallas_call(
        _norm_kernel,
        grid=(m // tm,),
        in_specs=[
            pl.BlockSpec((tm, d), lambda i: (i, 0)),
            pl.BlockSpec((1, d), lambda i: (0, 0)),
            _row_spec(scale, tm),
            _row_spec(shift, tm),
        ],
        out_specs=pl.BlockSpec((tm, d), lambda i: (i, 0)),
        out_shape=jax.ShapeDtypeStruct((m, d), out_dtype),
        compiler_params=_params(("arbitrary",)),
        name="adaln_norm",
    )(x, nw.reshape(1, d), scale, shift)


def _proj_acc(h_ref, w_ref, wbf_ref):
    @pl.when(pl.program_id(1) == 0)
    def _():
        wbf_ref[...] = w_ref[...].astype(BF16)

    return jnp.dot(h_ref[...], wbf_ref[...], preferred_element_type=F32)


def _proj_f32_kernel(h_ref, w_ref, o_ref, wbf_ref):
    o_ref[...] = _proj_acc(h_ref, w_ref, wbf_ref)


def _proj_bias_kernel(h_ref, w_ref, b_ref, o_ref, wbf_ref):
    o_ref[...] = _proj_acc(h_ref, w_ref, wbf_ref) + b_ref[...]


def _proj_bf16_kernel(h_ref, w_ref, o_ref, wbf_ref, *, scales):
    acc = _proj_acc(h_ref, w_ref, wbf_ref)
    j = pl.program_id(0)
    s = jnp.float32(scales[0])
    for t in range(1, len(scales)):
        s = jnp.where(j == t, jnp.float32(scales[t]), s)
    o_ref[...] = (acc * s).astype(BF16)


def _stack_out(prev_ref, o_ref, acc):
    if prev_ref is not None:
        o_ref[0:prev_ref.shape[0]] = prev_ref[...]
    o_ref[o_ref.shape[0] - 1] = acc


def _proj_k_kernel(h_ref, w_ref, *rest, has_prev):
    prev_ref = rest[0] if has_prev else None
    o_ref, obf_ref, wbf_ref = rest[-3:]
    acc = _proj_acc(h_ref, w_ref, wbf_ref)
    _stack_out(prev_ref, o_ref, acc)
    obf_ref[...] = acc.astype(BF16)


def _proj_vt_kernel(h_ref, w_ref, *rest, tk, has_prev):
    prev_ref = rest[0] if has_prev else None
    o_ref, ot_ref, wbf_ref = rest[-3:]
    acc = _proj_acc(h_ref, w_ref, wbf_ref)
    _stack_out(prev_ref, o_ref, acc)
    at = acc.T.astype(BF16)
    nh, nt = ot_ref.shape[0], ot_ref.shape[1]
    va = ot_ref.shape[2] - ONES_ROWS
    ones = (lax.broadcasted_iota(jnp.int32, (ONES_ROWS, tk), 0) == 0).astype(BF16)
    for hh in range(nh):
        for t in range(nt):
            ot_ref[hh, t, 0:va, :] = at[hh * va:(hh + 1) * va, t * tk:(t + 1) * tk]
            ot_ref[hh, t, va:va + ONES_ROWS, :] = ones


def _q_split(acc, axis, da, qk_scale):
    idx = lax.broadcasted_iota(jnp.int32, acc.shape, axis)
    first = (idx % (2 * da)) < da
    a = acc * qk_scale
    zero = jnp.zeros_like(a)
    return jnp.where(first, a, zero).astype(BF16), jnp.where(first, zero, a).astype(BF16)


def _proj_q_kernel(h_ref, w_ref, o1_ref, o2_ref, wbf_ref, *, da, qk_scale):
    acc = _proj_acc(h_ref, w_ref, wbf_ref)
    o1_ref[...], o2_ref[...] = _q_split(acc, 1, da, qk_scale)


def _proj_qt_kernel(h_ref, w_ref, o1_ref, o2_ref, wbf_ref, *, da, qk_scale):
    acc = _proj_acc(h_ref, w_ref, wbf_ref)
    o1_ref[...], o2_ref[...] = _q_split(acc.T, 0, da, qk_scale)


def _col_index(cols, j):
    idx = jnp.int32(cols[0])
    for t in range(1, len(cols)):
        idx = jnp.where(j == t, jnp.int32(cols[t]), idx)
    return idx


def _proj_call(body, h, w, cols, tn, tm, out_shapes, out_specs, extra_in=(), extra_specs=(), name="proj"):
    m, d = h.shape
    nj = len(cols)
    return pl.pallas_call(
        body,
        grid=(nj, m // tm),
        in_specs=[
            pl.BlockSpec((tm, d), lambda j, i: (i, 0)),
            pl.BlockSpec((d, tn), lambda j, i: (0, _col_index(cols, j))),
            *extra_specs,
        ],
        out_specs=out_specs,
        out_shape=out_shapes,
        scratch_shapes=[pltpu.VMEM((d, tn), BF16)],
        compiler_params=_params(("arbitrary", "arbitrary")),
        name=name,
    )(h, w, *extra_in)


def _proj_f32(h, w, cols, tn, tm):
    m = h.shape[0]
    return _proj_call(
        _proj_f32_kernel, h, w, cols, tn, tm,
        jax.ShapeDtypeStruct((len(cols), m, tn), F32),
        pl.BlockSpec((None, tm, tn), lambda j, i: (j, i, 0)), name="proj_f32")


def _proj_bf16(h, w, cols, scales, tn, tm):
    m = h.shape[0]
    return _proj_call(
        functools.partial(_proj_bf16_kernel, scales=tuple(scales)), h, w, cols, tn, tm,
        jax.ShapeDtypeStruct((len(cols), m, tn), BF16),
        pl.BlockSpec((None, tm, tn), lambda j, i: (j, i, 0)), name="proj_bf16")


def _proj_bias(h, w_small, bias, tm):
    m = h.shape[0]
    tn = w_small.shape[1]
    return _proj_call(
        _proj_bias_kernel, h, w_small, (0,), tn, tm,
        jax.ShapeDtypeStruct((m, tn), F32),
        pl.BlockSpec((tm, tn), lambda j, i: (i, 0)),
        extra_in=(bias,), extra_specs=(pl.BlockSpec((1, tn), lambda j, i: (0, 0)),), name="proj_gates")


def _stacked_specs(prev, m, tn, tm):
    n_prev = 0 if prev is None else prev.shape[0]
    shape = jax.ShapeDtypeStruct((n_prev + 1, m, tn), F32)
    spec = pl.BlockSpec((n_prev + 1, tm, tn), lambda j, i: (0, i, 0))
    extra_in = () if prev is None else (prev,)
    extra_specs = () if prev is None else (pl.BlockSpec((n_prev, tm, tn), lambda j, i: (0, i, 0)),)
    return shape, spec, extra_in, extra_specs


def _proj_k(h, w, col, tn, tm, prev):
    m = h.shape[0]
    shape, spec, extra_in, extra_specs = _stacked_specs(prev, m, tn, tm)
    return _proj_call(
        functools.partial(_proj_k_kernel, has_prev=prev is not None), h, w, (col,), tn, tm,
        (shape, jax.ShapeDtypeStruct((m, tn), BF16)),
        (spec, pl.BlockSpec((tm, tn), lambda j, i: (i, 0))),
        extra_in=extra_in, extra_specs=extra_specs, name="proj_k")


def _proj_vt(h, w, col, tn, tm, heads, tk, prev):
    m = h.shape[0]
    va = tn // heads
    shape, spec, extra_in, extra_specs = _stacked_specs(prev, m, tn, tm)
    return _proj_call(
        functools.partial(_proj_vt_kernel, tk=tk, has_prev=prev is not None), h, w, (col,), tn, tm,
        (shape, jax.ShapeDtypeStruct((heads, m // tk, va + ONES_ROWS, tk), BF16)),
        (spec, pl.BlockSpec((heads, tm // tk, va + ONES_ROWS, tk), lambda j, i: (0, i, 0, 0))),
        extra_in=extra_in, extra_specs=extra_specs, name="proj_vt")


def _proj_q(h, w, col, tn, tm, da, transposed):
    m = h.shape[0]
    qk_scale = float(da) ** -0.5
    if transposed:
        qk_scale *= math.log2(math.e)
        body = functools.partial(_proj_qt_kernel, da=da, qk_scale=qk_scale)
        shape = jax.ShapeDtypeStruct((tn, m), BF16)
        spec = pl.BlockSpec((tn, tm), lambda j, i: (0, i))
    else:
        body = functools.partial(_proj_q_kernel, da=da, qk_scale=qk_scale)
        shape = jax.ShapeDtypeStruct((m, tn), BF16)
        spec = pl.BlockSpec((tm, tn), lambda j, i: (i, 0))
    return _proj_call(body, h, w, (col,), tn, tm, (shape, shape), (spec, spec), name="proj_q")


def _attn_finish(o, z, sw, post):
    ms = jnp.mean(o * o, axis=-1, keepdims=True)
    on = (o * lax.rsqrt(ms + EPS)) * sw
    return (on * post) * _silu(z.astype(F32))


def _attn_prompt_kernel(sc_ref, q1_ref, q2_ref, k_ref, vt_ref, za_ref, sw_ref, o_ref,
                        m_ref, acc_ref, sa_ref, sb_ref, cma_ref, cmb_ref, *, tq, tk):
    i = pl.program_id(1)
    q_refs = (q1_ref, q2_ref)
    m_ref[...] = jnp.full(m_ref.shape, -jnp.inf, F32)
    acc_ref[...] = jnp.zeros(acc_ref.shape, F32)
    r = tq // tk
    va = o_ref.shape[1]

    def scores(j, c, lo):
        kt = k_ref[pl.ds(pl.multiple_of(j * tk, tk), tk), :]
        return jnp.dot(kt, q_refs[c][:, lo:], preferred_element_type=F32)

    def update(j, c, s, smax, cols):
        vt = vt_ref[j]
        m_old = m_ref[c, :, cols]
        m_new = jnp.maximum(m_old, smax)
        alpha = jnp.exp2(m_old - m_new)
        p = jnp.exp2(s - m_new)
        acc_ref[c, :, cols] = (alpha * acc_ref[c, :, cols]
                               + jnp.dot(vt, p.astype(BF16), preferred_element_type=F32))
        m_ref[c, :, cols] = m_new

    def fill(j, buf, lo=0):
        s_ref, cm_ref = buf
        for c in range(2):
            s = scores(j, c, lo)
            s_ref[c, :, lo:] = s
            cm_ref[c, :, lo:] = jnp.max(s, axis=0, keepdims=True)

    def drain(j, buf, lo=0, masked=False):
        s_ref, cm_ref = buf
        cols = slice(lo, tq)
        for c in range(2):
            s = s_ref[c, :, cols]
            if masked:
                kc = lax.broadcasted_iota(jnp.int32, s.shape, 0) // CHUNK
                qc = lax.broadcasted_iota(jnp.int32, s.shape, 1) // CHUNK
                s = jnp.where(kc <= qc, s, -jnp.inf)
                smax = jnp.max(s, axis=0, keepdims=True)
            else:
                smax = cm_ref[c, :, cols]
            update(j, c, s, smax, cols)

    n = i * r
    bufs = ((sa_ref, cma_ref), (sb_ref, cmb_ref))
    fill(0, bufs[0])

    def body(t, carry):
        j = 2 * t
        fill(j + 1, bufs[1])
        drain(j, bufs[0])
        fill(j + 2, bufs[0])
        drain(j + 1, bufs[1])
        return carry

    lax.fori_loop(0, n // 2, body, 0)
    for dd in range(r):
        if dd + 1 < r:
            fill(n + dd + 1, bufs[(dd + 1) % 2], (dd + 1) * tk)
        drain(n + dd, bufs[dd % 2], dd * tk, masked=True)

    lam = sc_ref[0, 0]
    post = sc_ref[0, 1]
    o = (acc_ref[0, 0:va, :] / acc_ref[0, va:va + 1, :]
         - lam * (acc_ref[1, 0:va, :] / acc_ref[1, va:va + 1, :]))
    o_ref[...] = _attn_finish(o.T, za_ref[...], sw_ref[...], post).astype(o_ref.dtype)


def _attn_prompt(scal, q1t, q2t, kbf, vt, fz, sw, heads, tq):
    dq, m = q1t.shape
    hd = dq // heads
    tk = vt.shape[3]
    hv = vt.shape[2]
    assert tq % (2 * tk) == 0 and m % tq == 0
    nk = m // tk
    return pl.pallas_call(
        functools.partial(_attn_prompt_kernel, tq=tq, tk=tk),
        grid=(heads, m // tq),
        in_specs=[
            pl.BlockSpec(memory_space=pltpu.SMEM),
            pl.BlockSpec((hd, tq), lambda h, i: (h, i)),
            pl.BlockSpec((hd, tq), lambda h, i: (h, i)),
            pl.BlockSpec((m, hd), lambda h, i: (0, h)),
            pl.BlockSpec((None, nk, hv, tk), lambda h, i: (h, 0, 0, 0)),
            pl.BlockSpec((None, tq, hd), lambda h, i: (0, i, h)),
            pl.BlockSpec((1, hd), lambda h, i: (0, 0)),
        ],
        out_specs=pl.BlockSpec((tq, hd), lambda h, i: (i, h)),
        out_shape=jax.ShapeDtypeStruct((m, dq), BF16),
        scratch_shapes=[
            pltpu.VMEM((2, 1, tq), F32),
            pltpu.VMEM((2, hv, tq), F32),
            pltpu.VMEM((2, tk, tq), F32),
            pltpu.VMEM((2, tk, tq), F32),
            pltpu.VMEM((2, 1, tq), F32),
            pltpu.VMEM((2, 1, tq), F32),
        ],
        compiler_params=_params(("arbitrary", "arbitrary")),
        name="diff_attn_prompt",
    )(scal, q1t, q2t, kbf, vt, fz, sw)


def _attn_sample_kernel(sc_ref, q1_ref, q2_ref, ck_ref, cv_ref, kn_ref, vn_ref, za_ref, sw_ref, o_ref,
                        *, heads):
    hd = o_ref.shape[1] // heads
    lam = sc_ref[0, 0]
    post = sc_ref[0, 1]
    for h in range(heads):
        hs = slice(h * hd, (h + 1) * hd)
        ck = ck_ref[:, hs]
        cv = cv_ref[:, hs]
        kn = kn_ref[:, hs].astype(BF16)
        vn = vn_ref[:, hs].astype(BF16)
        outs = []
        for q_ref in (q1_ref, q2_ref):
            q = q_ref[:, hs]
            sc = lax.dot_general(q, ck, NT_DIMS, preferred_element_type=F32)
            sn = lax.dot_general(q, kn, NT_DIMS, preferred_element_type=F32)
            mx = jnp.maximum(jnp.max(sc, axis=-1, keepdims=True), jnp.max(sn, axis=-1, keepdims=True))
            pc = jnp.exp(sc - mx)
            pn = jnp.exp(sn - mx)
            l = jnp.sum(pc, axis=-1, keepdims=True) + jnp.sum(pn, axis=-1, keepdims=True)
            acc = (jnp.dot(pc.astype(BF16), cv, preferred_element_type=F32)
                   + jnp.dot(pn.astype(BF16), vn, preferred_element_type=F32))
            outs.append(acc / l)
        o = outs[0] - lam * outs[1]
        o_ref[:, hs] = _attn_finish(o, za_ref[:, hs], sw_ref[...], post).astype(o_ref.dtype)


def _attn_sample(scal, q1, q2, ck, cv, kv, fz, sw, heads, t):
    m, dq = q1.shape
    hd = dq // heads
    b, p, _ = ck.shape
    return pl.pallas_call(
        functools.partial(_attn_sample_kernel, heads=heads),
        grid=(b,),
        in_specs=[
            pl.BlockSpec(memory_space=pltpu.SMEM),
            pl.BlockSpec((t, dq), lambda bb: (bb, 0)),
            pl.BlockSpec((t, dq), lambda bb: (bb, 0)),
            pl.BlockSpec((None, p, dq), lambda bb: (bb, 0, 0)),
            pl.BlockSpec((None, p, dq), lambda bb: (bb, 0, 0)),
            pl.BlockSpec((None, t, dq), lambda bb: (0, bb, 0)),
            pl.BlockSpec((None, t, dq), lambda bb: (1, bb, 0)),
            pl.BlockSpec((None, t, dq), lambda bb: (0, bb, 0)),
            pl.BlockSpec((1, hd), lambda bb: (0, 0)),
        ],
        out_specs=pl.BlockSpec((t, dq), lambda bb: (bb, 0)),
        out_shape=jax.ShapeDtypeStruct((m, dq), BF16),
        compiler_params=_params(("arbitrary",)),
        name="diff_attn_sample",
    )(scal, q1, q2, ck, cv, kv, kv, fz, sw)


def _mlstm_kernel(q_ref, k_ref, v_ref, ob_ref, zb_ref, g_ref, c0_ref, n0_ref, m0_ref, w_ref,
                  hb_ref, c_out, n_out, m_out, ct_scr, m_scr, *, L, cpb, H, DH):
    t = pl.program_id(1)

    @pl.when(t == 0)
    def _():
        for h in range(H):
            ct_scr[h, 0:DH, :] = c0_ref[0, h].T
            ct_scr[h, DH:DH + SUBLANES, :] = jnp.zeros((SUBLANES, DH), F32)
            ct_scr[h, DH:DH + 1, :] = n0_ref[0, h:h + 1, :]
        m_scr[...] = m0_ref[0]

    src = lax.broadcasted_iota(jnp.int32, (L, L), 0)
    tgt = lax.broadcasted_iota(jnp.int32, (L, L), 1)
    visible = src <= tgt
    tril = (tgt <= src).astype(F32)
    sel = (lax.broadcasted_iota(jnp.int32, (2 * H, LANES), 0)
           == lax.broadcasted_iota(jnp.int32, (2 * H, LANES), 1)).astype(F32)
    lane = lax.broadcasted_iota(jnp.int32, (1, LANES), 1)

    for cc in range(cpb):
        r0 = cc * L
        g = g_ref[r0:r0 + L, :]
        lf = jax.nn.log_sigmoid(g)
        bcum = jnp.dot(tril, lf, precision=HIGHEST, preferred_element_type=F32)
        x = jnp.where(lane < H, g, bcum)
        xt = lax.dot_general(sel, x, NT_DIMS, precision=HIGHEST, preferred_element_type=F32)
        csx = g - pltpu.roll(bcum, LANES - H, 1)
        m_all = m_scr[...]
        m_next = m_all
        for h in range(H):
            b_row = xt[H + h:H + h + 1, :]
            cs = jnp.broadcast_to(csx[:, h:h + 1], (L, DH))
            m0 = m_all[:, h:h + 1]
            dmat = jnp.where(visible, b_row + cs[:, 0:L], -jnp.inf)
            m_row = jnp.maximum(b_row + m0, jnp.max(dmat, axis=0, keepdims=True))
            wgt = jnp.exp(dmat - m_row)
            inter = jnp.exp(b_row + m0 - m_row)
            hs = slice(h * DH, (h + 1) * DH)
            qh = q_ref[r0:r0 + L, hs]
            kh = k_ref[r0:r0 + L, hs]
            vh = v_ref[r0:r0 + L, hs]
            st = lax.dot_general(kh, qh, NT_DIMS, preferred_element_type=F32) * wgt
            ctn = ct_scr[h]
            carried = lax.dot_general(ctn.astype(BF16), qh, NT_DIMS,
                                      preferred_element_type=F32)
            num = (lax.dot_general(vh, st.astype(BF16), TN_DIMS, preferred_element_type=F32)
                   + inter * carried[0:DH, :])
            den = jnp.sum(st, axis=0, keepdims=True) + inter * carried[DH:DH + 1, :]
            denom = jnp.maximum(jnp.abs(den), jnp.exp(-m_row))
            hh = (num / denom).T
            hh = jax.nn.sigmoid(ob_ref[r0:r0 + L, hs].astype(F32)) * hh
            ms = jnp.mean(hh * hh, axis=-1, keepdims=True)
            hn = (hh * lax.rsqrt(ms + EPS)) * w_ref[:, hs]
            hb_ref[r0:r0 + L, hs] = (hn * _silu(zb_ref[r0:r0 + L, hs].astype(F32))).astype(hb_ref.dtype)
            b_last = b_row[:, L - 1:L]
            m_last = m_row[:, L - 1:L]
            ws = jnp.exp(cs + (b_last - m_last))
            decay = jnp.exp(b_last + m0 - m_last)
            wv = (ws * vh.astype(F32)).astype(BF16)
            ct_scr[h, 0:DH, :] = (decay * ctn[0:DH, :]
                                  + lax.dot_general(wv, kh, TN_DIMS, preferred_element_type=F32))
            ct_scr[h, DH:DH + 1, :] = (decay * ctn[DH:DH + 1, :]
                                       + jnp.sum(ws * kh.astype(F32), axis=0, keepdims=True))
            m_next = jnp.where(lane == h, m_last, m_next)
        m_scr[...] = m_next

    @pl.when(t == pl.num_programs(1) - 1)
    def _():
        for h in range(H):
            c_out[0, h] = ct_scr[h, 0:DH, :].T
            n_out[0, h:h + 1, :] = ct_scr[h, DH:DH + 1, :]
        m_out[0] = m_scr[...]


def _mlstm(qkv, fz, gates, c0, n0, m0, mh_w, batch, L, cpb):
    _, m, wb = qkv.shape
    _, H, DH, _ = c0.shape
    rows = L * cpb
    nt = m // batch // rows
    blk = lambda which: pl.BlockSpec((None, rows, wb), lambda b, t: (which, b * nt + t, 0))
    return pl.pallas_call(
        functools.partial(_mlstm_kernel, L=L, cpb=cpb, H=H, DH=DH),
        grid=(batch, nt),
        in_specs=[
            blk(0), blk(1), blk(2),
            blk(1), blk(2),
            pl.BlockSpec((rows, LANES), lambda b, t: (b * nt + t, 0)),
            pl.BlockSpec((1, H, DH, DH), lambda b, t: (b, 0, 0, 0)),
            pl.BlockSpec((1, H, DH), lambda b, t: (b, 0, 0)),
            pl.BlockSpec((1, 1, LANES), lambda b, t: (b, 0, 0)),
            pl.BlockSpec((1, wb), lambda b, t: (0, 0)),
        ],
        out_specs=(
            pl.BlockSpec((rows, wb), lambda b, t: (b * nt + t, 0)),
            pl.BlockSpec((1, H, DH, DH), lambda b, t: (b, 0, 0, 0)),
            pl.BlockSpec((1, H, DH), lambda b, t: (b, 0, 0)),
            pl.BlockSpec((1, 1, LANES), lambda b, t: (b, 0, 0)),
        ),
        out_shape=(
            jax.ShapeDtypeStruct((m, wb), BF16),
            jax.ShapeDtypeStruct((batch, H, DH, DH), F32),
            jax.ShapeDtypeStruct((batch, H, DH), F32),
            jax.ShapeDtypeStruct((batch, 1, LANES), F32),
        ),
        scratch_shapes=[
            pltpu.VMEM((H, DH + SUBLANES, DH), F32),
            pltpu.VMEM((1, LANES), F32),
        ],
        compiler_params=_params(("arbitrary", "arbitrary")),
        name="mlstm_chunks",
    )(qkv, qkv, qkv, fz, fz, gates, c0, n0, m0, mh_w)


def _outproj_kernel(*refs, n_in, emit_x):
    a_refs = refs[:n_in]
    w_ref, x_ref, g_ref, nw_ref, sc_ref, sh_ref = refs[n_in:n_in + 6]
    outs = refs[n_in + 6:]
    wbf_ref = outs[-1]
    h_ref = outs[-2]

    @pl.when(pl.program_id(0) == 0)
    def _():
        wbf_ref[...] = w_ref[...].astype(BF16)

    acc = None
    off = 0
    for a_ref in a_refs:
        kk = a_ref.shape[1]
        part = jnp.dot(a_ref[...], wbf_ref[off:off + kk, :], preferred_element_type=F32)
        acc = part if acc is None else acc + part
        off += kk
    xn = x_ref[...] + g_ref[...] * acc
    if emit_x:
        outs[0][...] = xn
    ms = jnp.mean(xn * xn, axis=-1, keepdims=True)
    y = (xn * lax.rsqrt(ms + EPS)) * nw_ref[...]
    h_ref[...] = (y * (1.0 + sc_ref[...]) + sh_ref[...]).astype(h_ref.dtype)


def _outproj(acts, w, x, gate, nw, scale, shift, h_dtype, emit_x, tm):
    m, d = x.shape
    kt = w.shape[0]
    row = pl.BlockSpec((tm, d), lambda i: (i, 0))
    out_shapes = [jax.ShapeDtypeStruct((m, d), h_dtype)]
    out_specs = [row]
    if emit_x:
        out_shapes.insert(0, jax.ShapeDtypeStruct((m, d), F32))
        out_specs.insert(0, row)
    res = pl.pallas_call(
        functools.partial(_outproj_kernel, n_in=len(acts), emit_x=emit_x),
        grid=(m // tm,),
        in_specs=[
            *[pl.BlockSpec((tm, a.shape[1]), lambda i: (i, 0)) for a in acts],
            pl.BlockSpec((kt, d), lambda i: (0, 0)),
            row,
            _row_spec(gate, tm),
            pl.BlockSpec((1, d), lambda i: (0, 0)),
            _row_spec(scale, tm),
            _row_spec(shift, tm),
        ],
        out_specs=out_specs,
        out_shape=out_shapes,
        scratch_shapes=[pltpu.VMEM((kt, d), BF16)],
        compiler_params=_params(("arbitrary",)),
        name="outproj_residual",
    )(*acts, w, x, gate, nw.reshape(1, d), scale, shift)
    return (res[0], res[1]) if emit_x else (None, res[0])


def _rglru_kernel(xr_ref, zr_ref, cw_ref, cb_ref, wg_ref, ba_ref, bx_ref, lam_ref, buf0_ref, h0_ref,
                  y_ref, cs_ref, hl_ref, xbuf, a_scr, u_scr, hcar, *, tt, nblk, bw):
    t = pl.program_id(1)
    pad = SUBLANES

    @pl.when(t == 0)
    def _():
        xbuf[0:pad, :] = buf0_ref[0]
        hcar[...] = h0_ref[0]

    xbuf[pad:pad + tt, :] = xr_ref[...]
    xc = cb_ref[...] + cw_ref[CONV_W - 1:CONV_W, :] * xbuf[pad:pad + tt, :]
    for j in range(1, CONV_W):
        xc = xc + cw_ref[CONV_W - 1 - j:CONV_W - j, :] * xbuf[pad - j:pad - j + tt, :]

    neg_c = -C_RG * jax.nn.softplus(-lam_ref[...])
    for hb in range(nblk):
        cs = slice(hb * bw, (hb + 1) * bw)
        xb = xc[:, cs]
        gg = jnp.dot(xb.astype(BF16), wg_ref[hb], preferred_element_type=F32)
        r = _sigmoid_t(gg[:, :bw] + ba_ref[:, cs])
        ig = _sigmoid_t(gg[:, bw:] + bx_ref[:, cs])
        log_a = neg_c[:, cs] * r
        th = jnp.tanh(log_a)
        one_minus_a2 = (-2.0 * th) / (1.0 - th)
        a_scr[:, cs] = jnp.exp(log_a)
        u_scr[:, cs] = jnp.sqrt(one_minus_a2) * (ig * xb)

    rowi = lax.broadcasted_iota(jnp.int32, (SUBLANES, a_scr.shape[1]), 0)

    def group(gi, hc):
        r0 = pl.multiple_of(gi * SUBLANES, SUBLANES)
        a = a_scr[pl.ds(r0, SUBLANES), :]
        u = u_scr[pl.ds(r0, SUBLANES), :]
        for dd in (1, 2, 4):
            ok = rowi >= dd
            u = jnp.where(ok, a * pltpu.roll(u, dd, 0) + u, u)
            a = jnp.where(ok, a * pltpu.roll(a, dd, 0), a)
        hgrp = u + a * hc
        u_scr[pl.ds(r0, SUBLANES), :] = hgrp
        return hgrp[SUBLANES - 1:SUBLANES, :]

    hc = lax.fori_loop(0, tt // SUBLANES, group, hcar[...])
    hcar[...] = hc
    zr = zr_ref[...]
    y_ref[...] = (u_scr[...] * (zr * _sigmoid_t(zr))).astype(y_ref.dtype)
    xbuf[0:pad, :] = xbuf[tt:tt + pad, :]

    @pl.when(t == pl.num_programs(1) - 1)
    def _():
        cs_ref[0] = xbuf[0:pad, :]
        hl_ref[0] = hc


def _rglru(xz, cw, cb, wg, ba, bx, lam, buf0, h0, batch, tt):
    _, m, dr = xz.shape
    nblk, bw, _ = wg.shape
    nt = m // batch // tt
    vec = lambda: pl.BlockSpec((1, dr), lambda b, t: (0, 0))
    return pl.pallas_call(
        functools.partial(_rglru_kernel, tt=tt, nblk=nblk, bw=bw),
        grid=(batch, nt),
        in_specs=[
            pl.BlockSpec((None, tt, dr), lambda b, t: (0, b * nt + t, 0)),
            pl.BlockSpec((None, tt, dr), lambda b, t: (1, b * nt + t, 0)),
            pl.BlockSpec((CONV_W, dr), lambda b, t: (0, 0)),
            vec(),
            pl.BlockSpec((nblk, bw, 2 * bw), lambda b, t: (0, 0, 0)),
            vec(), vec(), vec(),
            pl.BlockSpec((1, SUBLANES, dr), lambda b, t: (b, 0, 0)),
            pl.BlockSpec((1, 1, dr), lambda b, t: (b, 0, 0)),
        ],
        out_specs=(
            pl.BlockSpec((tt, dr), lambda b, t: (b * nt + t, 0)),
            pl.BlockSpec((1, SUBLANES, dr), lambda b, t: (b, 0, 0)),
            pl.BlockSpec((1, 1, dr), lambda b, t: (b, 0, 0)),
        ),
        out_shape=(
            jax.ShapeDtypeStruct((m, dr), BF16),
            jax.ShapeDtypeStruct((batch, SUBLANES, dr), F32),
            jax.ShapeDtypeStruct((batch, 1, dr), F32),
        ),
        scratch_shapes=[
            pltpu.VMEM((tt + SUBLANES, dr), F32),
            pltpu.VMEM((tt, dr), F32),
            pltpu.VMEM((tt, dr), F32),
            pltpu.VMEM((1, dr), F32),
        ],
        compiler_params=_params(("arbitrary", "arbitrary")),
        name="rglru_block",
    )(xz, xz, cw, cb, wg, ba, bx, lam, buf0, h0)


def _tile(m, pref):
    t = min(m, pref)
    assert m % t == 0
    return t


def _trunk(x3, mods, p, cache, prompt):
    b, t, d = x3.shape
    m = b * t
    depth = p["norm_w"].shape[0]
    x = x3.reshape(m, d)
    tm = _tile(m, 1024)
    va = p["subln_w"].shape[1]
    da = va // 2
    wb = p["mh_norm_w"].shape[1]
    tn = p["w_out_even"].shape[1] - wb
    assert tn == wb
    heads = tn // va
    hb_heads = p["b_gates_even"].shape[1] // 2
    tk = 512
    tq = _tile(m, 1024)
    k_list, v_list, mc_list, mn_list, mm_list, conv_list, hl_list = [], [], [], [], [], [], []
    k_stack = v_stack = None

    def rows(v):
        if b == 1:
            return v
        return jnp.broadcast_to(v[:, None, :], (b, t, d)).reshape(m, d)

    zero = jnp.zeros((1, d), F32)

    def close_layer(li, acts, w_out, x, gate):
        if li + 1 < depth:
            nshift, nscale, _ = mods[li + 1]
            return _outproj(acts, w_out, x, rows(gate), p["norm_w"][li + 1], rows(nscale), rows(nshift),
                            BF16, True, tm)
        return _outproj(acts, w_out, x, rows(gate), p["final_w"], zero, zero, F32, False, tm)

    shift, scale, _ = mods[0]
    h = _norm(x, p["norm_w"][0], rows(scale), rows(shift), BF16, tm)
    for li in range(depth):
        gate = mods[li][2]
        if li % 2 == 0:
            e = li // 2
            w_in = p["w_in_even"][e]
            lam_init = 0.8 - 0.6 * math.exp(-0.3 * li)
            lam = (jnp.exp(jnp.sum(p["lambda_q1"][e] * p["lambda_k1"][e]))
                   - jnp.exp(jnp.sum(p["lambda_q2"][e] * p["lambda_k2"][e])) + lam_init)
            scal = jnp.stack([lam, jnp.float32(1.0 - lam_init)]).reshape(1, 2).astype(F32)
            sw = p["subln_w"][e].reshape(1, -1)
            fz = _proj_bf16(h, w_in, (3, 7, 8), (1.0, 1.0, 1.0), tn, tm)
            qkv_b = _proj_bf16(h, w_in, (4, 5, 6), (1.0, float(LANES) ** -0.5, 1.0), tn, tm)
            ng = 2 * hb_heads
            w_g = jnp.pad(w_in[:, 9 * tn:9 * tn + ng], ((0, 0), (0, LANES - ng)))
            b_g = jnp.pad(p["b_gates_even"][e], (0, LANES - ng)).reshape(1, LANES)
            gates = _proj_bias(h, w_g, b_g, tm)
            if prompt:
                q1, q2 = _proj_q(h, w_in, 0, tn, tm, da, True)
                k_stack, k_b = _proj_k(h, w_in, 1, tn, tm, k_stack)
                v_stack, v_t = _proj_vt(h, w_in, 2, tn, tm, heads, tk, v_stack)
                oa = _attn_prompt(scal, q1, q2, k_b, v_t, fz, sw, heads, tq)
                dh = wb // hb_heads
                c0 = jnp.zeros((b, hb_heads, dh, dh), F32)
                n0 = jnp.zeros((b, hb_heads, dh), F32)
                m0 = jnp.zeros((b, 1, LANES), F32)
                L, cpb = _tile(t, LANES), 4
            else:
                q1, q2 = _proj_q(h, w_in, 0, tn, tm, da, False)
                kv = _proj_f32(h, w_in, (1, 2), tn, tm)
                k_f, v_f = kv[0], kv[1]
                ck = cache["k"][e].astype(BF16).reshape(b, -1, heads * 2 * da)
                cv = cache["v"][e].astype(BF16).reshape(b, -1, heads * 2 * da)
                oa = _attn_sample(scal, q1, q2, ck, cv, kv, fz, sw, heads, t)
                c0 = cache["c"][e]
                n0 = cache["n"][e]
                m0 = jnp.pad(cache["m"][e], ((0, 0), (0, LANES - hb_heads))).reshape(b, 1, LANES)
                L, cpb = t, 1
            hb, c_n, n_n, m_n = _mlstm(qkv_b, fz, gates, c0, n0, m0,
                                       p["mh_norm_w"][e].reshape(1, -1), b, L, cpb)
            if not prompt:
                k_list.append(k_f.reshape(b, t, heads, 2, da))
                v_list.append(v_f.reshape(b, t, heads, 2 * da))
            mc_list.append(c_n)
            mn_list.append(n_n)
            mm_list.append(m_n[:, 0, :hb_heads])
            x, h = close_layer(li, [oa, hb], p["w_out_even"][e], x, gate)
        else:
            o = li // 2
            w_in = p["w_in_odd"][o]
            dr = w_in.shape[1] // 2
            xz = _proj_f32(h, w_in, (0, 1), dr, _tile(m, 512))
            wg = jnp.concatenate([p["rg_wa"][o], p["rg_wx"][o]], axis=-1).astype(BF16)
            if cache is None:
                buf0 = jnp.zeros((b, SUBLANES, dr), F32)
                h0 = jnp.zeros((b, 1, dr), F32)
            else:
                buf0 = jnp.pad(cache["conv"][o], ((0, 0), (SUBLANES - (CONV_W - 1), 0), (0, 0)))
                h0 = cache["h"][o].reshape(b, 1, dr)
            y, cs, hl = _rglru(xz, p["conv_w"][o], p["conv_b"][o].reshape(1, dr), wg,
                               p["rg_ba"][o].reshape(1, dr), p["rg_bx"][o].reshape(1, dr),
                               p["rg_lambda"][o].reshape(1, dr), buf0, h0, b, _tile(t, 256))
            conv_list.append(cs[:, SUBLANES - (CONV_W - 1):, :])
            hl_list.append(hl[:, 0, :])
            x, h = close_layer(li, [y], p["w_out_odd"][o], x, gate)
    y = h.reshape(b, t, d)
    if prompt:
        n_even = k_stack.shape[0]
        kv_out = [k_stack.reshape(n_even, b, t, heads, 2, da), v_stack.reshape(n_even, b, t, heads, 2 * da)]
    else:
        kv_out = [jnp.stack(k_list), jnp.stack(v_list)]
    ev = kv_out + [jnp.stack(s) for s in (mc_list, mn_list, mm_list)]
    od = [jnp.stack(s) for s in (conv_list, hl_list)]
    return y, ev, od


def kernel(x_prompt, x_sample, c_prompt, c_sample, cache_k, cache_v, state_mlstm_c, state_mlstm_n, state_mlstm_m, state_conv, state_rglru, norm_w, w_ada, b_ada, w_in_even, b_gates_even, lambda_q1, lambda_k1, lambda_q2, lambda_k2, subln_w, mh_norm_w, w_out_even, w_in_odd, conv_w, conv_b, rg_wa, rg_ba, rg_wx, rg_bx, rg_lambda, w_out_odd, final_w):
    p = dict(norm_w=norm_w, w_in_even=w_in_even, b_gates_even=b_gates_even,
             lambda_q1=lambda_q1, lambda_k1=lambda_k1, lambda_q2=lambda_q2, lambda_k2=lambda_k2,
             subln_w=subln_w, mh_norm_w=mh_norm_w, w_out_even=w_out_even, w_in_odd=w_in_odd,
             conv_w=conv_w, conv_b=conv_b, rg_wa=rg_wa, rg_ba=rg_ba, rg_wx=rg_wx, rg_bx=rg_bx,
             rg_lambda=rg_lambda, w_out_odd=w_out_odd, final_w=final_w)
    depth, d = norm_w.shape
    bp, bs = c_prompt.shape[0], c_sample.shape[0]
    rpad = -(bp + bs) % SUBLANES
    c_all = jnp.pad(jnp.concatenate([c_prompt, c_sample], axis=0), ((0, rpad), (0, 0)))
    mod = _modulation(c_all, w_ada, b_ada)

    def mods_for(lo, hi):
        return [tuple(mod[li, lo:hi, i * d:(i + 1) * d] for i in range(3)) for li in range(depth)]

    y_p, ev_p, od_p = _trunk(x_prompt, mods_for(0, bp), p, None, True)
    cache = dict(k=cache_k, v=cache_v, c=state_mlstm_c, n=state_mlstm_n, m=state_mlstm_m,
                 conv=state_conv, h=state_rglru)
    y_s, ev_s, od_s = _trunk(x_sample, mods_for(bp, bp + bs), p, cache, False)
    return (y_p, y_s, *ev_p, *od_p, *ev_s, *od_s)
```

```python
import functools
import math

import jax
import jax.numpy as jnp
from jax import lax
from jax.experimental import pallas as pl
from jax.experimental.pallas import tpu as pltpu

F32 = jnp.float32
BF16 = jnp.bfloat16
EPS = 1e-6
CHUNK = 64
C_RG = 8.0
CONV_W = 4
LANES = 128
SUBLANES = 8
ONES_ROWS = 16
NT_DIMS = (((1,), (1,)), ((), ()))
TN_DIMS = (((0,), (0,)), ((), ()))
HIGHEST = lax.Precision.HIGHEST
VMEM_LIMIT = 56 * 1024 * 1024


def _params(sem):
    return pltpu.CompilerParams(dimension_semantics=sem, vmem_limit_bytes=VMEM_LIMIT)


def _silu(z):
    return z * jax.nn.sigmoid(z)


def _sigmoid_t(z):
    return 0.5 * jnp.tanh(0.5 * z) + 0.5


def _mod_kernel(c_ref, w_ref, b_ref, o_ref):
    sc = _silu(c_ref[...]).astype(BF16)
    o_ref[0] = jnp.dot(sc, w_ref[0].astype(BF16), preferred_element_type=F32) + b_ref[0]


def _modulation(c_all, w_ada, b_ada):
    depth, d, n = w_ada.shape
    r = c_all.shape[0]
    tn = 1024
    return pl.pallas_call(
        _mod_kernel,
        grid=(depth, n // tn),
        in_specs=[
            pl.BlockSpec((r, d), lambda l, j: (0, 0)),
            pl.BlockSpec((1, d, tn), lambda l, j: (l, 0, j)),
            pl.BlockSpec((1, 1, tn), lambda l, j: (l, 0, j)),
        ],
        out_specs=pl.BlockSpec((1, r, tn), lambda l, j: (l, 0, j)),
        out_shape=jax.ShapeDtypeStruct((depth, r, n), F32),
        compiler_params=_params(("arbitrary", "arbitrary")),
        name="adaln_mod",
    )(c_all, w_ada, b_ada.reshape(depth, 1, n))


def _norm_kernel(x_ref, nw_ref, sc_ref, sh_ref, o_ref):
    x = x_ref[...]
    ms = jnp.mean(x * x, axis=-1, keepdims=True)
    y = (x * lax.rsqrt(ms + EPS)) * nw_ref[...]
    o_ref[...] = (y * (1.0 + sc_ref[...]) + sh_ref[...]).astype(o_ref.dtype)


def _row_spec(arr, tm):
    if arr.shape[0] == 1:
        return pl.BlockSpec((1, arr.shape[1]), lambda i: (0, 0))
    return pl.BlockSpec((tm, arr.shape[1]), lambda i: (i, 0))


def _norm(x, nw, scale, shift, out_dtype, tm):
    m, d = x.shape
    return pl.pallas_call(
        _norm_kernel,
        grid=(m // tm,),
        in_specs=[
            pl.BlockSpec((tm, d), lambda i: (i, 0)),
            pl.BlockSpec((1, d), lambda i: (0, 0)),
            _row_spec(scale, tm),
            _row_spec(shift, tm),
        ],
        out_specs=pl.BlockSpec((tm, d), lambda i: (i, 0)),
        out_shape=jax.ShapeDtypeStruct((m, d), out_dtype),
        compiler_params=_params(("arbitrary",)),
        name="adaln_norm",
    )(x, nw.reshape(1, d), scale, shift)


def _proj_acc(h_ref, w_ref, wbf_ref):
    @pl.when(pl.program_id(1) == 0)
    def _():
        wbf_ref[...] = w_ref[...].astype(BF16)

    return jnp.dot(h_ref[...], wbf_ref[...], preferred_element_type=F32)


def _proj_f32_kernel(h_ref, w_ref, o_ref, wbf_ref):
    o_ref[...] = _proj_acc(h_ref, w_ref, wbf_ref)


def _proj_bias_kernel(h_ref, w_ref, b_ref, o_ref, wbf_ref):
    o_ref[...] = _proj_acc(h_ref, w_ref, wbf_ref) + b_ref[...]


def _proj_bf16_kernel(h_ref, w_ref, o_ref, wbf_ref, *, scales):
    acc = _proj_acc(h_ref, w_ref, wbf_ref)
    j = pl.program_id(0)
    s = jnp.float32(scales[0])
    for t in range(1, len(scales)):
        s = jnp.where(j == t, jnp.float32(scales[t]), s)
    o_ref[...] = (acc * s).astype(BF16)


def _stack_out(prev_ref, o_ref, acc):
    if prev_ref is not None:
        o_ref[0:prev_ref.shape[0]] = prev_ref[...]
    o_ref[o_ref.shape[0] - 1] = acc


def _proj_k_kernel(h_ref, w_ref, *rest, has_prev):
    prev_ref = rest[0] if has_prev else None
    o_ref, obf_ref, wbf_ref = rest[-3:]
    acc = _proj_acc(h_ref, w_ref, wbf_ref)
    _stack_out(prev_ref, o_ref, acc)
    obf_ref[...] = acc.astype(BF16)


def _proj_vt_kernel(h_ref, w_ref, *rest, tk, has_prev):
    prev_ref = rest[0] if has_prev else None
    o_ref, ot_ref, wbf_ref = rest[-3:]
    acc = _proj_acc(h_ref, w_ref, wbf_ref)
    _stack_out(prev_ref, o_ref, acc)
    at = acc.T.astype(BF16)
    nh, nt = ot_ref.shape[0], ot_ref.shape[1]
    va = ot_ref.shape[2] - ONES_ROWS
    ones = (lax.broadcasted_iota(jnp.int32, (ONES_ROWS, tk), 0) == 0).astype(BF16)
    for hh in range(nh):
        for t in range(nt):
            ot_ref[hh, t, 0:va, :] = at[hh * va:(hh + 1) * va, t * tk:(t + 1) * tk]
            ot_ref[hh, t, va:va + ONES_ROWS, :] = ones


def _q_split(acc, axis, da, qk_scale):
    idx = lax.broadcasted_iota(jnp.int32, acc.shape, axis)
    first = (idx % (2 * da)) < da
    a = acc * qk_scale
    zero = jnp.zeros_like(a)
    return jnp.where(first, a, zero).astype(BF16), jnp.where(first, zero, a).astype(BF16)


def _proj_q_kernel(h_ref, w_ref, o1_ref, o2_ref, wbf_ref, *, da, qk_scale):
    acc = _proj_acc(h_ref, w_ref, wbf_ref)
    o1_ref[...], o2_ref[...] = _q_split(acc, 1, da, qk_scale)


def _proj_qt_kernel(h_ref, w_ref, o1_ref, o2_ref, wbf_ref, *, da, qk_scale):
    acc = _proj_acc(h_ref, w_ref, wbf_ref)
    o1_ref[...], o2_ref[...] = _q_split(acc.T, 0, da, qk_scale)


def _col_index(cols, j):
    idx = jnp.int32(cols[0])
    for t in range(1, len(cols)):
        idx = jnp.where(j == t, jnp.int32(cols[t]), idx)
    return idx


def _proj_call(body, h, w, cols, tn, tm, out_shapes, out_specs, extra_in=(), extra_specs=(), name="proj"):
    m, d = h.shape
    nj = len(cols)
    return pl.pallas_call(
        body,
        grid=(nj, m // tm),
        in_specs=[
            pl.BlockSpec((tm, d), lambda j, i: (i, 0)),
            pl.BlockSpec((d, tn), lambda j, i: (0, _col_index(cols, j))),
            *extra_specs,
        ],
        out_specs=out_specs,
        out_shape=out_shapes,
        scratch_shapes=[pltpu.VMEM((d, tn), BF16)],
        compiler_params=_params(("arbitrary", "arbitrary")),
        name=name,
    )(h, w, *extra_in)


def _proj_f32(h, w, cols, tn, tm):
    m = h.shape[0]
    return _proj_call(
        _proj_f32_kernel, h, w, cols, tn, tm,
        jax.ShapeDtypeStruct((len(cols), m, tn), F32),
        pl.BlockSpec((None, tm, tn), lambda j, i: (j, i, 0)), name="proj_f32")


def _proj_bf16(h, w, cols, scales, tn, tm):
    m = h.shape[0]
    return _proj_call(
        functools.partial(_proj_bf16_kernel, scales=tuple(scales)), h, w, cols, tn, tm,
        jax.ShapeDtypeStruct((len(cols), m, tn), BF16),
        pl.BlockSpec((None, tm, tn), lambda j, i: (j, i, 0)), name="proj_bf16")


def _proj_bias(h, w_small, bias, tm):
    m = h.shape[0]
    tn = w_small.shape[1]
    return _proj_call(
        _proj_bias_kernel, h, w_small, (0,), tn, tm,
        jax.ShapeDtypeStruct((m, tn), F32),
        pl.BlockSpec((tm, tn), lambda j, i: (i, 0)),
        extra_in=(bias,), extra_specs=(pl.BlockSpec((1, tn), lambda j, i: (0, 0)),), name="proj_gates")


def _stacked_specs(prev, m, tn, tm):
    n_prev = 0 if prev is None else prev.shape[0]
    shape = jax.ShapeDtypeStruct((n_prev + 1, m, tn), F32)
    spec = pl.BlockSpec((n_prev + 1, tm, tn), lambda j, i: (0, i, 0))
    extra_in = () if prev is None else (prev,)
    extra_specs = () if prev is None else (pl.BlockSpec((n_prev, tm, tn), lambda j, i: (0, i, 0)),)
    return shape, spec, extra_in, extra_specs


def _proj_k(h, w, col, tn, tm, prev):
    m = h.shape[0]
    shape, spec, extra_in, extra_specs = _stacked_specs(prev, m, tn, tm)
    return _proj_call(
        functools.partial(_proj_k_kernel, has_prev=prev is not None), h, w, (col,), tn, tm,
        (shape, jax.ShapeDtypeStruct((m, tn), BF16)),
        (spec, pl.BlockSpec((tm, tn), lambda j, i: (i, 0))),
        extra_in=extra_in, extra_specs=extra_specs, name="proj_k")


def _proj_vt(h, w, col, tn, tm, heads, tk, prev):
    m = h.shape[0]
    va = tn // heads
    shape, spec, extra_in, extra_specs = _stacked_specs(prev, m, tn, tm)
    return _proj_call(
        functools.partial(_proj_vt_kernel, tk=tk, has_prev=prev is not None), h, w, (col,), tn, tm,
        (shape, jax.ShapeDtypeStruct((heads, m // tk, va + ONES_ROWS, tk), BF16)),
        (spec, pl.BlockSpec((heads, tm // tk, va + ONES_ROWS, tk), lambda j, i: (0, i, 0, 0))),
        extra_in=extra_in, extra_specs=extra_specs, name="proj_vt")


def _proj_q(h, w, col, tn, tm, da, transposed):
    m = h.shape[0]
    qk_scale = float(da) ** -0.5
    if transposed:
        qk_scale *= math.log2(math.e)
        body = functools.partial(_proj_qt_kernel, da=da, qk_scale=qk_scale)
        shape = jax.ShapeDtypeStruct((tn, m), BF16)
        spec = pl.BlockSpec((tn, tm), lambda j, i: (0, i))
    else:
        body = functools.partial(_proj_q_kernel, da=da, qk_scale=qk_scale)
        shape = jax.ShapeDtypeStruct((m, tn), BF16)
        spec = pl.BlockSpec((tm, tn), lambda j, i: (i, 0))
    return _proj_call(body, h, w, (col,), tn, tm, (shape, shape), (spec, spec), name="proj_q")


def _attn_finish(o, z, sw, post):
    ms = jnp.mean(o * o, axis=-1, keepdims=True)
    on = (o * lax.rsqrt(ms + EPS)) * sw
    return (on * post) * _silu(z.astype(F32))


def _attn_prompt_kernel(sc_ref, q1_ref, q2_ref, k_ref, vt_ref, za_ref, sw_ref, o_ref,
                        m_ref, acc_ref, sa_ref, sb_ref, cma_ref, cmb_ref, *, tq, tk):
    i = pl.program_id(1)
    q_refs = (q1_ref, q2_ref)
    m_ref[...] = jnp.full(m_ref.shape, -jnp.inf, F32)
    acc_ref[...] = jnp.zeros(acc_ref.shape, F32)
    r = tq // tk
    va = o_ref.shape[1]

    def scores(j, c, lo):
        kt = k_ref[pl.ds(pl.multiple_of(j * tk, tk), tk), :]
        return jnp.dot(kt, q_refs[c][:, lo:], preferred_element_type=F32)

    def update(j, c, s, smax, cols):
        vt = vt_ref[j]
        m_old = m_ref[c, :, cols]
        m_new = jnp.maximum(m_old, smax)
        alpha = jnp.exp2(m_old - m_new)
        p = jnp.exp2(s - m_new)
        acc_ref[c, :, cols] = (alpha * acc_ref[c, :, cols]
                               + jnp.dot(vt, p.astype(BF16), preferred_element_type=F32))
        m_ref[c, :, cols] = m_new

    def fill(j, buf, lo=0):
        s_ref, cm_ref = buf
        for c in range(2):
            s = scores(j, c, lo)
            s_ref[c, :, lo:] = s
            cm_ref[c, :, lo:] = jnp.max(s, axis=0, keepdims=True)

    def drain(j, buf, lo=0, masked=False):
        s_ref, cm_ref = buf
        cols = slice(lo, tq)
        for c in range(2):
            s = s_ref[c, :, cols]
            if masked:
                kc = lax.broadcasted_iota(jnp.int32, s.shape, 0) // CHUNK
                qc = lax.broadcasted_iota(jnp.int32, s.shape, 1) // CHUNK
                s = jnp.where(kc <= qc, s, -jnp.inf)
                smax = jnp.max(s, axis=0, keepdims=True)
            else:
                smax = cm_ref[c, :, cols]
            update(j, c, s, smax, cols)

    n = i * r
    bufs = ((sa_ref, cma_ref), (sb_ref, cmb_ref))
    fill(0, bufs[0])

    def pair(j):
        fill(j + 1, bufs[1])
        drain(j, bufs[0])
        fill(j + 2, bufs[0])
        drain(j + 1, bufs[1])

    def body(t, carry):
        pair(4 * t)
        pair(4 * t + 2)
        return carry

    npairs = n // 2
    lax.fori_loop(0, npairs // 2, body, 0)

    @pl.when(npairs % 2 == 1)
    def _():
        pair(n - 2)

    for dd in range(r):
        if dd + 1 < r:
            fill(n + dd + 1, bufs[(dd + 1) % 2], (dd + 1) * tk)
        drain(n + dd, bufs[dd % 2], dd * tk, masked=True)

    lam = sc_ref[0, 0]
    post = sc_ref[0, 1]
    o = (acc_ref[0, 0:va, :] / acc_ref[0, va:va + 1, :]
         - lam * (acc_ref[1, 0:va, :] / acc_ref[1, va:va + 1, :]))
    o_ref[...] = _attn_finish(o.T, za_ref[...], sw_ref[...], post).astype(o_ref.dtype)


def _attn_prompt(scal, q1t, q2t, kbf, vt, fz, sw, heads, tq):
    dq, m = q1t.shape
    hd = dq // heads
    tk = vt.shape[3]
    hv = vt.shape[2]
    assert tq % (2 * tk) == 0 and m % tq == 0
    nk = m // tk
    return pl.pallas_call(
        functools.partial(_attn_prompt_kernel, tq=tq, tk=tk),
        grid=(heads, m // tq),
        in_specs=[
            pl.BlockSpec(memory_space=pltpu.SMEM),
            pl.BlockSpec((hd, tq), lambda h, i: (h, i)),
            pl.BlockSpec((hd, tq), lambda h, i: (h, i)),
            pl.BlockSpec((m, hd), lambda h, i: (0, h)),
            pl.BlockSpec((None, nk, hv, tk), lambda h, i: (h, 0, 0, 0)),
            pl.BlockSpec((None, tq, hd), lambda h, i: (0, i, h)),
            pl.BlockSpec((1, hd), lambda h, i: (0, 0)),
        ],
        out_specs=pl.BlockSpec((tq, hd), lambda h, i: (i, h)),
        out_shape=jax.ShapeDtypeStruct((m, dq), BF16),
        scratch_shapes=[
            pltpu.VMEM((2, 1, tq), F32),
            pltpu.VMEM((2, hv, tq), F32),
            pltpu.VMEM((2, tk, tq), F32),
            pltpu.VMEM((2, tk, tq), F32),
            pltpu.VMEM((2, 1, tq), F32),
            pltpu.VMEM((2, 1, tq), F32),
        ],
        compiler_params=_params(("arbitrary", "arbitrary")),
        name="diff_attn_prompt",
    )(scal, q1t, q2t, kbf, vt, fz, sw)


def _attn_sample_kernel(sc_ref, q1_ref, q2_ref, ck_ref, cv_ref, kn_ref, vn_ref, za_ref, sw_ref, o_ref,
                        *, heads):
    hd = o_ref.shape[1] // heads
    lam = sc_ref[0, 0]
    post = sc_ref[0, 1]
    for h in range(heads):
        hs = slice(h * hd, (h + 1) * hd)
        ck = ck_ref[:, hs]
        cv = cv_ref[:, hs]
        kn = kn_ref[:, hs].astype(BF16)
        vn = vn_ref[:, hs].astype(BF16)
        outs = []
        for q_ref in (q1_ref, q2_ref):
            q = q_ref[:, hs]
            sc = lax.dot_general(q, ck, NT_DIMS, preferred_element_type=F32)
            sn = lax.dot_general(q, kn, NT_DIMS, preferred_element_type=F32)
            mx = jnp.maximum(jnp.max(sc, axis=-1, keepdims=True), jnp.max(sn, axis=-1, keepdims=True))
            pc = jnp.exp(sc - mx)
            pn = jnp.exp(sn - mx)
            l = jnp.sum(pc, axis=-1, keepdims=True) + jnp.sum(pn, axis=-1, keepdims=True)
            acc = (jnp.dot(pc.astype(BF16), cv, preferred_element_type=F32)
                   + jnp.dot(pn.astype(BF16), vn, preferred_element_type=F32))
            outs.append(acc / l)
        o = outs[0] - lam * outs[1]
        o_ref[:, hs] = _attn_finish(o, za_ref[:, hs], sw_ref[...], post).astype(o_ref.dtype)


def _attn_sample(scal, q1, q2, ck, cv, kv, fz, sw, heads, t):
    m, dq = q1.shape
    hd = dq // heads
    b, p, _ = ck.shape
    return pl.pallas_call(
        functools.partial(_attn_sample_kernel, heads=heads),
        grid=(b,),
        in_specs=[
            pl.BlockSpec(memory_space=pltpu.SMEM),
            pl.BlockSpec((t, dq), lambda bb: (bb, 0)),
            pl.BlockSpec((t, dq), lambda bb: (bb, 0)),
            pl.BlockSpec((None, p, dq), lambda bb: (bb, 0, 0)),
            pl.BlockSpec((None, p, dq), lambda bb: (bb, 0, 0)),
            pl.BlockSpec((None, t, dq), lambda bb: (0, bb, 0)),
            pl.BlockSpec((None, t, dq), lambda bb: (1, bb, 0)),
            pl.BlockSpec((None, t, dq), lambda bb: (0, bb, 0)),
            pl.BlockSpec((1, hd), lambda bb: (0, 0)),
        ],
        out_specs=pl.BlockSpec((t, dq), lambda bb: (bb, 0)),
        out_shape=jax.ShapeDtypeStruct((m, dq), BF16),
        compiler_params=_params(("arbitrary",)),
        name="diff_attn_sample",
    )(scal, q1, q2, ck, cv, kv, kv, fz, sw)


def _mlstm_kernel(q_ref, k_ref, v_ref, ob_ref, zb_ref, g_ref, c0_ref, n0_ref, m0_ref, w_ref,
                  hb_ref, c_out, n_out, m_out, ct_scr, m_scr, *, L, cpb, H, DH):
    t = pl.program_id(1)

    @pl.when(t == 0)
    def _():
        for h in range(H):
            ct_scr[h, 0:DH, :] = c0_ref[0, h].T
            ct_scr[h, DH:DH + SUBLANES, :] = jnp.zeros((SUBLANES, DH), F32)
            ct_scr[h, DH:DH + 1, :] = n0_ref[0, h:h + 1, :]
        m_scr[...] = m0_ref[0]

    src = lax.broadcasted_iota(jnp.int32, (L, L), 0)
    tgt = lax.broadcasted_iota(jnp.int32, (L, L), 1)
    visible = src <= tgt
    tril = (tgt <= src).astype(F32)
    sel = (lax.broadcasted_iota(jnp.int32, (2 * H, LANES), 0)
           == lax.broadcasted_iota(jnp.int32, (2 * H, LANES), 1)).astype(F32)
    lane = lax.broadcasted_iota(jnp.int32, (1, LANES), 1)

    for cc in range(cpb):
        r0 = cc * L
        g = g_ref[r0:r0 + L, :]
        lf = jax.nn.log_sigmoid(g)
        bcum = jnp.dot(tril, lf, precision=HIGHEST, preferred_element_type=F32)
        x = jnp.where(lane < H, g, bcum)
        xt = lax.dot_general(sel, x, NT_DIMS, precision=HIGHEST, preferred_element_type=F32)
        csx = g - pltpu.roll(bcum, LANES - H, 1)
        m_all = m_scr[...]
        m_next = m_all
        for h in range(H):
            b_row = xt[H + h:H + h + 1, :]
            cs = jnp.broadcast_to(csx[:, h:h + 1], (L, DH))
            m0 = m_all[:, h:h + 1]
            dmat = jnp.where(visible, b_row + cs[:, 0:L], -jnp.inf)
            m_row = jnp.maximum(b_row + m0, jnp.max(dmat, axis=0, keepdims=True))
            wgt = jnp.exp(dmat - m_row)
            inter = jnp.exp(b_row + m0 - m_row)
            hs = slice(h * DH, (h + 1) * DH)
            qh = q_ref[r0:r0 + L, hs]
            kh = k_ref[r0:r0 + L, hs]
            vh = v_ref[r0:r0 + L, hs]
            st = lax.dot_general(kh, qh, NT_DIMS, preferred_element_type=F32) * wgt
            ctn = ct_scr[h]
            carried = lax.dot_general(ctn.astype(BF16), qh, NT_DIMS,
                                      preferred_element_type=F32)
            num = (lax.dot_general(vh, st.astype(BF16), TN_DIMS, preferred_element_type=F32)
                   + inter * carried[0:DH, :])
            den = jnp.sum(st, axis=0, keepdims=True) + inter * carried[DH:DH + 1, :]
            denom = jnp.maximum(jnp.abs(den), jnp.exp(-m_row))
            hh = (num / denom).T
            hh = jax.nn.sigmoid(ob_ref[r0:r0 + L, hs].astype(F32)) * hh
            ms = jnp.mean(hh * hh, axis=-1, keepdims=True)
            hn = (hh * lax.rsqrt(ms + EPS)) * w_ref[:, hs]
            hb_ref[r0:r0 + L, hs] = (hn * _silu(zb_ref[r0:r0 + L, hs].astype(F32))).astype(hb_ref.dtype)
            b_last = b_row[:, L - 1:L]
            m_last = m_row[:, L - 1:L]
            ws = jnp.exp(cs + (b_last - m_last))
            decay = jnp.exp(b_last + m0 - m_last)
            wv = (ws * vh.astype(F32)).astype(BF16)
            ct_scr[h, 0:DH, :] = (decay * ctn[0:DH, :]
                                  + lax.dot_general(wv, kh, TN_DIMS, preferred_element_type=F32))
            ct_scr[h, DH:DH + 1, :] = (decay * ctn[DH:DH + 1, :]
                                       + jnp.sum(ws * kh.astype(F32), axis=0, keepdims=True))
            m_next = jnp.where(lane == h, m_last, m_next)
        m_scr[...] = m_next

    @pl.when(t == pl.num_programs(1) - 1)
    def _():
        for h in range(H):
            c_out[0, h] = ct_scr[h, 0:DH, :].T
            n_out[0, h:h + 1, :] = ct_scr[h, DH:DH + 1, :]
        m_out[0] = m_scr[...]


def _mlstm(qkv, fz, gates, c0, n0, m0, mh_w, batch, L, cpb):
    _, m, wb = qkv.shape
    _, H, DH, _ = c0.shape
    rows = L * cpb
    nt = m // batch // rows
    blk = lambda which: pl.BlockSpec((None, rows, wb), lambda b, t: (which, b * nt + t, 0))
    return pl.pallas_call(
        functools.partial(_mlstm_kernel, L=L, cpb=cpb, H=H, DH=DH),
        grid=(batch, nt),
        in_specs=[
            blk(0), blk(1), blk(2),
            blk(1), blk(2),
            pl.BlockSpec((rows, LANES), lambda b, t: (b * nt + t, 0)),
            pl.BlockSpec((1, H, DH, DH), lambda b, t: (b, 0, 0, 0)),
            pl.BlockSpec((1, H, DH), lambda b, t: (b, 0, 0)),
            pl.BlockSpec((1, 1, LANES), lambda b, t: (b, 0, 0)),
            pl.BlockSpec((1, wb), lambda b, t: (0, 0)),
        ],
        out_specs=(
            pl.BlockSpec((rows, wb), lambda b, t: (b * nt + t, 0)),
            pl.BlockSpec((1, H, DH, DH), lambda b, t: (b, 0, 0, 0)),
            pl.BlockSpec((1, H, DH), lambda b, t: (b, 0, 0)),
            pl.BlockSpec((1, 1, LANES), lambda b, t: (b, 0, 0)),
        ),
        out_shape=(
            jax.ShapeDtypeStruct((m, wb), BF16),
            jax.ShapeDtypeStruct((batch, H, DH, DH), F32),
            jax.ShapeDtypeStruct((batch, H, DH), F32),
            jax.ShapeDtypeStruct((batch, 1, LANES), F32),
        ),
        scratch_shapes=[
            pltpu.VMEM((H, DH + SUBLANES, DH), F32),
            pltpu.VMEM((1, LANES), F32),
        ],
        compiler_params=_params(("arbitrary", "arbitrary")),
        name="mlstm_chunks",
    )(qkv, qkv, qkv, fz, fz, gates, c0, n0, m0, mh_w)


def _outproj_kernel(*refs, n_in, emit_x):
    a_refs = refs[:n_in]
    w_ref, x_ref, g_ref, nw_ref, sc_ref, sh_ref = refs[n_in:n_in + 6]
    outs = refs[n_in + 6:]
    wbf_ref = outs[-1]
    h_ref = outs[-2]

    @pl.when(pl.program_id(0) == 0)
    def _():
        wbf_ref[...] = w_ref[...].astype(BF16)

    acc = None
    off = 0
    for a_ref in a_refs:
        kk = a_ref.shape[1]
        part = jnp.dot(a_ref[...], wbf_ref[off:off + kk, :], preferred_element_type=F32)
        acc = part if acc is None else acc + part
        off += kk
    xn = x_ref[...] + g_ref[...] * acc
    if emit_x:
        outs[0][...] = xn
    ms = jnp.mean(xn * xn, axis=-1, keepdims=True)
    y = (xn * lax.rsqrt(ms + EPS)) * nw_ref[...]
    h_ref[...] = (y * (1.0 + sc_ref[...]) + sh_ref[...]).astype(h_ref.dtype)


def _outproj(acts, w, x, gate, nw, scale, shift, h_dtype, emit_x, tm):
    m, d = x.shape
    kt = w.shape[0]
    row = pl.BlockSpec((tm, d), lambda i: (i, 0))
    out_shapes = [jax.ShapeDtypeStruct((m, d), h_dtype)]
    out_specs = [row]
    if emit_x:
        out_shapes.insert(0, jax.ShapeDtypeStruct((m, d), F32))
        out_specs.insert(0, row)
    res = pl.pallas_call(
        functools.partial(_outproj_kernel, n_in=len(acts), emit_x=emit_x),
        grid=(m // tm,),
        in_specs=[
            *[pl.BlockSpec((tm, a.shape[1]), lambda i: (i, 0)) for a in acts],
            pl.BlockSpec((kt, d), lambda i: (0, 0)),
            row,
            _row_spec(gate, tm),
            pl.BlockSpec((1, d), lambda i: (0, 0)),
            _row_spec(scale, tm),
            _row_spec(shift, tm),
        ],
        out_specs=out_specs,
        out_shape=out_shapes,
        scratch_shapes=[pltpu.VMEM((kt, d), BF16)],
        compiler_params=_params(("arbitrary",)),
        name="outproj_residual",
    )(*acts, w, x, gate, nw.reshape(1, d), scale, shift)
    return (res[0], res[1]) if emit_x else (None, res[0])


def _rglru_kernel(xr_ref, zr_ref, cw_ref, cb_ref, wg_ref, ba_ref, bx_ref, lam_ref, buf0_ref, h0_ref,
                  y_ref, cs_ref, hl_ref, xbuf, a_scr, u_scr, hcar, *, tt, nblk, bw):
    t = pl.program_id(1)
    pad = SUBLANES

    @pl.when(t == 0)
    def _():
        xbuf[0:pad, :] = buf0_ref[0]
        hcar[...] = h0_ref[0]

    xbuf[pad:pad + tt, :] = xr_ref[...]
    xc = cb_ref[...] + cw_ref[CONV_W - 1:CONV_W, :] * xbuf[pad:pad + tt, :]
    for j in range(1, CONV_W):
        xc = xc + cw_ref[CONV_W - 1 - j:CONV_W - j, :] * xbuf[pad - j:pad - j + tt, :]

    neg_c = -C_RG * jax.nn.softplus(-lam_ref[...])
    for hb in range(nblk):
        cs = slice(hb * bw, (hb + 1) * bw)
        xb = xc[:, cs]
        gg = jnp.dot(xb.astype(BF16), wg_ref[hb], preferred_element_type=F32)
        r = _sigmoid_t(gg[:, :bw] + ba_ref[:, cs])
        ig = _sigmoid_t(gg[:, bw:] + bx_ref[:, cs])
        log_a = neg_c[:, cs] * r
        th = jnp.tanh(log_a)
        one_minus_a2 = (-2.0 * th) / (1.0 - th)
        a_scr[:, cs] = jnp.exp(log_a)
        u_scr[:, cs] = jnp.sqrt(one_minus_a2) * (ig * xb)

    rowi = lax.broadcasted_iota(jnp.int32, (SUBLANES, a_scr.shape[1]), 0)

    def group(gi, hc):
        r0 = pl.multiple_of(gi * SUBLANES, SUBLANES)
        a = a_scr[pl.ds(r0, SUBLANES), :]
        u = u_scr[pl.ds(r0, SUBLANES), :]
        for dd in (1, 2, 4):
            ok = rowi >= dd
            u = jnp.where(ok, a * pltpu.roll(u, dd, 0) + u, u)
            a = jnp.where(ok, a * pltpu.roll(a, dd, 0), a)
        hgrp = u + a * hc
        u_scr[pl.ds(r0, SUBLANES), :] = hgrp
        return hgrp[SUBLANES - 1:SUBLANES, :]

    hc = lax.fori_loop(0, tt // SUBLANES, group, hcar[...])
    hcar[...] = hc
    zr = zr_ref[...]
    y_ref[...] = (u_scr[...] * (zr * _sigmoid_t(zr))).astype(y_ref.dtype)
    xbuf[0:pad, :] = xbuf[tt:tt + pad, :]

    @pl.when(t == pl.num_programs(1) - 1)
    def _():
        cs_ref[0] = xbuf[0:pad, :]
        hl_ref[0] = hc


def _rglru(xz, cw, cb, wg, ba, bx, lam, buf0, h0, batch, tt):
    _, m, dr = xz.shape
    nblk, bw, _ = wg.shape
    nt = m // batch // tt
    vec = lambda: pl.BlockSpec((1, dr), lambda b, t: (0, 0))
    return pl.pallas_call(
        functools.partial(_rglru_kernel, tt=tt, nblk=nblk, bw=bw),
        grid=(batch, nt),
        in_specs=[
            pl.BlockSpec((None, tt, dr), lambda b, t: (0, b * nt + t, 0)),
            pl.BlockSpec((None, tt, dr), lambda b, t: (1, b * nt + t, 0)),
            pl.BlockSpec((CONV_W, dr), lambda b, t: (0, 0)),
            vec(),
            pl.BlockSpec((nblk, bw, 2 * bw), lambda b, t: (0, 0, 0)),
            vec(), vec(), vec(),
            pl.BlockSpec((1, SUBLANES, dr), lambda b, t: (b, 0, 0)),
            pl.BlockSpec((1, 1, dr), lambda b, t: (b, 0, 0)),
        ],
        out_specs=(
            pl.BlockSpec((tt, dr), lambda b, t: (b * nt + t, 0)),
            pl.BlockSpec((1, SUBLANES, dr), lambda b, t: (b, 0, 0)),
            pl.BlockSpec((1, 1, dr), lambda b, t: (b, 0, 0)),
        ),
        out_shape=(
            jax.ShapeDtypeStruct((m, dr), BF16),
            jax.ShapeDtypeStruct((batch, SUBLANES, dr), F32),
            jax.ShapeDtypeStruct((batch, 1, dr), F32),
        ),
        scratch_shapes=[
            pltpu.VMEM((tt + SUBLANES, dr), F32),
            pltpu.VMEM((tt, dr), F32),
            pltpu.VMEM((tt, dr), F32),
            pltpu.VMEM((1, dr), F32),
        ],
        compiler_params=_params(("arbitrary", "arbitrary")),
        name="rglru_block",
    )(xz, xz, cw, cb, wg, ba, bx, lam, buf0, h0)


def _tile(m, pref):
    t = min(m, pref)
    assert m % t == 0
    return t


def _trunk(x3, mods, p, cache, prompt):
    b, t, d = x3.shape
    m = b * t
    depth = p["norm_w"].shape[0]
    x = x3.reshape(m, d)
    tm = _tile(m, 1024)
    va = p["subln_w"].shape[1]
    da = va // 2
    wb = p["mh_norm_w"].shape[1]
    tn = p["w_out_even"].shape[1] - wb
    assert tn == wb
    heads = tn // va
    hb_heads = p["b_gates_even"].shape[1] // 2
    tk = 512
    tq = _tile(m, 1024)
    k_list, v_list, mc_list, mn_list, mm_list, conv_list, hl_list = [], [], [], [], [], [], []
    k_stack = v_stack = None

    def rows(v):
        if b == 1:
            return v
        return jnp.broadcast_to(v[:, None, :], (b, t, d)).reshape(m, d)

    zero = jnp.zeros((1, d), F32)

    def close_layer(li, acts, w_out, x, gate):
        if li + 1 < depth:
            nshift, nscale, _ = mods[li + 1]
            return _outproj(acts, w_out, x, rows(gate), p["norm_w"][li + 1], rows(nscale), rows(nshift),
                            BF16, True, tm)
        return _outproj(acts, w_out, x, rows(gate), p["final_w"], zero, zero, F32, False, tm)

    shift, scale, _ = mods[0]
    h = _norm(x, p["norm_w"][0], rows(scale), rows(shift), BF16, tm)
    for li in range(depth):
        gate = mods[li][2]
        if li % 2 == 0:
            e = li // 2
            w_in = p["w_in_even"][e]
            lam_init = 0.8 - 0.6 * math.exp(-0.3 * li)
            lam = (jnp.exp(jnp.sum(p["lambda_q1"][e] * p["lambda_k1"][e]))
                   - jnp.exp(jnp.sum(p["lambda_q2"][e] * p["lambda_k2"][e])) + lam_init)
            scal = jnp.stack([lam, jnp.float32(1.0 - lam_init)]).reshape(1, 2).astype(F32)
            sw = p["subln_w"][e].reshape(1, -1)
            fz = _proj_bf16(h, w_in, (3, 7, 8), (1.0, 1.0, 1.0), tn, tm)
            qkv_b = _proj_bf16(h, w_in, (4, 5, 6), (1.0, float(LANES) ** -0.5, 1.0), tn, tm)
            ng = 2 * hb_heads
            w_g = jnp.pad(w_in[:, 9 * tn:9 * tn + ng], ((0, 0), (0, LANES - ng)))
            b_g = jnp.pad(p["b_gates_even"][e], (0, LANES - ng)).reshape(1, LANES)
            gates = _proj_bias(h, w_g, b_g, tm)
            if prompt:
                q1, q2 = _proj_q(h, w_in, 0, tn, tm, da, True)
                k_stack, k_b = _proj_k(h, w_in, 1, tn, tm, k_stack)
                v_stack, v_t = _proj_vt(h, w_in, 2, tn, tm, heads, tk, v_stack)
                oa = _attn_prompt(scal, q1, q2, k_b, v_t, fz, sw, heads, tq)
                dh = wb // hb_heads
                c0 = jnp.zeros((b, hb_heads, dh, dh), F32)
                n0 = jnp.zeros((b, hb_heads, dh), F32)
                m0 = jnp.zeros((b, 1, LANES), F32)
                L, cpb = _tile(t, LANES), 4
            else:
                q1, q2 = _proj_q(h, w_in, 0, tn, tm, da, False)
                kv = _proj_f32(h, w_in, (1, 2), tn, tm)
                k_f, v_f = kv[0], kv[1]
                ck = cache["k"][e].astype(BF16).reshape(b, -1, heads * 2 * da)
                cv = cache["v"][e].astype(BF16).reshape(b, -1, heads * 2 * da)
                oa = _attn_sample(scal, q1, q2, ck, cv, kv, fz, sw, heads, t)
                c0 = cache["c"][e]
                n0 = cache["n"][e]
                m0 = jnp.pad(cache["m"][e], ((0, 0), (0, LANES - hb_heads))).reshape(b, 1, LANES)
                L, cpb = t, 1
            hb, c_n, n_n, m_n = _mlstm(qkv_b, fz, gates, c0, n0, m0,
                                       p["mh_norm_w"][e].reshape(1, -1), b, L, cpb)
            if not prompt:
                k_list.append(k_f.reshape(b, t, heads, 2, da))
                v_list.append(v_f.reshape(b, t, heads, 2 * da))
            mc_list.append(c_n)
            mn_list.append(n_n)
            mm_list.append(m_n[:, 0, :hb_heads])
            x, h = close_layer(li, [oa, hb], p["w_out_even"][e], x, gate)
        else:
            o = li // 2
            w_in = p["w_in_odd"][o]
            dr = w_in.shape[1] // 2
            xz = _proj_f32(h, w_in, (0, 1), dr, _tile(m, 512))
            wg = jnp.concatenate([p["rg_wa"][o], p["rg_wx"][o]], axis=-1).astype(BF16)
            if cache is None:
                buf0 = jnp.zeros((b, SUBLANES, dr), F32)
                h0 = jnp.zeros((b, 1, dr), F32)
            else:
                buf0 = jnp.pad(cache["conv"][o], ((0, 0), (SUBLANES - (CONV_W - 1), 0), (0, 0)))
                h0 = cache["h"][o].reshape(b, 1, dr)
            y, cs, hl = _rglru(xz, p["conv_w"][o], p["conv_b"][o].reshape(1, dr), wg,
                               p["rg_ba"][o].reshape(1, dr), p["rg_bx"][o].reshape(1, dr),
                               p["rg_lambda"][o].reshape(1, dr), buf0, h0, b, _tile(t, 256))
            conv_list.append(cs[:, SUBLANES - (CONV_W - 1):, :])
            hl_list.append(hl[:, 0, :])
            x, h = close_layer(li, [y], p["w_out_odd"][o], x, gate)
    y = h.reshape(b, t, d)
    if prompt:
        n_even = k_stack.shape[0]
        kv_out = [k_stack.reshape(n_even, b, t, heads, 2, da), v_stack.reshape(n_even, b, t, heads, 2 * da)]
    else:
        kv_out = [jnp.stack(k_list), jnp.stack(v_list)]
    ev = kv_out + [jnp.stack(s) for s in (mc_list, mn_list, mm_list)]
    od = [jnp.stack(s) for s in (conv_list, hl_list)]
    return y, ev, od


def kernel(x_prompt, x_sample, c_prompt, c_sample, cache_k, cache_v, state_mlstm_c, state_mlstm_n, state_mlstm_m, state_conv, state_rglru, norm_w, w_ada, b_ada, w_in_even, b_gates_even, lambda_q1, lambda_k1, lambda_q2, lambda_k2, subln_w, mh_norm_w, w_out_even, w_in_odd, conv_w, conv_b, rg_wa, rg_ba, rg_wx, rg_bx, rg_lambda, w_out_odd, final_w):
    p = dict(norm_w=norm_w, w_in_even=w_in_even, b_gates_even=b_gates_even,
             lambda_q1=lambda_q1, lambda_k1=lambda_k1, lambda_q2=lambda_q2, lambda_k2=lambda_k2,
             subln_w=subln_w, mh_norm_w=mh_norm_w, w_out_even=w_out_even, w_in_odd=w_in_odd,
             conv_w=conv_w, conv_b=conv_b, rg_wa=rg_wa, rg_ba=rg_ba, rg_wx=rg_wx, rg_bx=rg_bx,
             rg_lambda=rg_lambda, w_out_odd=w_out_odd, final_w=final_w)
    depth, d = norm_w.shape
    bp, bs = c_prompt.shape[0], c_sample.shape[0]
    rpad = -(bp + bs) % SUBLANES
    c_all = jnp.pad(jnp.concatenate([c_prompt, c_sample], axis=0), ((0, rpad), (0, 0)))
    mod = _modulation(c_all, w_ada, b_ada)

    def mods_for(lo, hi):
        return [tuple(mod[li, lo:hi, i * d:(i + 1) * d] for i in range(3)) for li in range(depth)]

    y_p, ev_p, od_p = _trunk(x_prompt, mods_for(0, bp), p, None, True)
    cache = dict(k=cache_k, v=cache_v, c=state_mlstm_c, n=state_mlstm_n, m=state_mlstm_m,
                 conv=state_conv, h=state_rglru)
    y_s, ev_s, od_s = _trunk(x_sample, mods_for(bp, bp + bs), p, cache, False)
    return (y_p, y_s, *ev_p, *od_p, *ev_s, *od_s)
```
